```python
import math
import jax
import jax.numpy as jnp
from jax import lax
import numpy as np

D_MODEL = 1024
BATCH = 1
SEQ = 16384
DEPTH = 4

GRID_W = 64
CTX_LEN = 256
N_BRANCH = 3
BRANCH_WIDTH = 512
EPS = 1e-6

LRU_WIDTH = 512
LRU_HEADS = 8
LRU_HEAD_DIM = LRU_WIDTH // LRU_HEADS
LRU_C = 8.0
CONV_W = 4
CONV_LEFT = 2
CONV_RIGHT = CONV_W - 1 - CONV_LEFT

RWKV_WIDTH = 512
RWKV_HEAD_DIM = 64
RWKV_HEADS = RWKV_WIDTH // RWKV_HEAD_DIM
DECAY_LORA = 64
ICLR_LORA = 64
GATE_LORA = 128
RWKV_IN = 3 * RWKV_WIDTH + 2 * DECAY_LORA + 2 * ICLR_LORA + GATE_LORA
RWKV_GN_EPS = 64e-5

S5_WIDTH = 512
S5_GROUP = 16
S5_GROUPS = S5_WIDTH // S5_GROUP
S5_STATE = 64

N_IN = 2 * LRU_WIDTH + RWKV_IN + S5_WIDTH + N_BRANCH * D_MODEL

FFN_HIDDEN = -(-8 * D_MODEL // (3 * 256)) * 256

kernel_name = 'hybrid_rglru_rwkv7_s5_prefix_ctx_dit'


def rms_norm(x, g):
    xf = x.astype(jnp.float32)
    y = xf * lax.rsqrt(jnp.mean(xf * xf, axis=-1, keepdims=True) + EPS)
    return (y * g.astype(jnp.float32)).astype(x.dtype)


def modulate(xn, shift, scale):
    return xn * (1.0 + scale) + shift


def to_col_major(t, rows):
    b, l, ch = t.shape
    return t.reshape(b, rows, GRID_W, ch).transpose(0, 2, 1, 3).reshape(b, l, ch)


def to_raster(t, rows):
    b, l, ch = t.shape
    return t.reshape(b, GRID_W, rows, ch).transpose(0, 2, 1, 3).reshape(b, l, ch)


def conv_centred(x, w, bias):
    l = x.shape[1]
    xp = jnp.pad(x, ((0, 0), (CONV_LEFT, CONV_RIGHT), (0, 0)))
    out = bias + w[0] * xp[:, 0:l]
    for j in range(1, CONV_W):
        out = out + w[j] * xp[:, j:j + l]
    return out


def _lin_combine(e1, e2):
    a1, b1 = e1
    a2, b2 = e2
    return a1 * a2, a2 * b1 + b2


def linear_scan(a, b, h0, reverse):
    j = -1 if reverse else 0
    b = b.at[:, j].add(a[:, j] * h0)
    _, h = lax.associative_scan(_lin_combine, (a, b), reverse=reverse, axis=1)
    return h, h[:, 0] if reverse else h[:, -1]


def _complex_combine(e1, e2):
    ar1, ai1, br1, bi1 = e1
    ar2, ai2, br2, bi2 = e2
    return (ar2 * ar1 - ai2 * ai1, ar2 * ai1 + ai2 * ar1,
            ar2 * br1 - ai2 * bi1 + br2, ar2 * bi1 + ai2 * br1 + bi2)


def complex_linear_scan(a_re, a_im, b_re, b_im, h0_re, h0_im, reverse):
    j = -1 if reverse else 0
    b_re = b_re.at[:, j].add(a_re[:, j] * h0_re - a_im[:, j] * h0_im)
    b_im = b_im.at[:, j].add(a_re[:, j] * h0_im + a_im[:, j] * h0_re)
    _, _, h_re, h_im = lax.associative_scan(_complex_combine, (a_re, a_im, b_re, b_im), reverse=reverse, axis=1)
    k = 0 if reverse else -1
    return h_re, h_im, h_re[:, k], h_im[:, k]


def rglru_direction(xc, w_a, b_a, w_x, b_x, lam, h0, reverse):
    b, l, _ = xc.shape
    xh = xc.reshape(b, l, LRU_HEADS, LRU_HEAD_DIM)
    gate_r = jax.nn.sigmoid(jnp.einsum('blhi,hij->blhj', xh, w_a).reshape(b, l, LRU_WIDTH) + b_a)
    gate_i = jax.nn.sigmoid(jnp.einsum('blhi,hij->blhj', xh, w_x).reshape(b, l, LRU_WIDTH) + b_x)
    log_a = -LRU_C * gate_r * jax.nn.softplus(-lam)
    a = jnp.exp(log_a)
    u = jnp.sqrt(-jnp.expm1(2.0 * log_a)) * (gate_i * xc)
    return linear_scan(a, u, h0, reverse)


def mixer_rglru(xa, ga, conv_w, conv_b, w_a, b_a, w_x, b_x, lam, h0):
    xc = conv_centred(xa, conv_w, conv_b)
    hs, fins = [], []
    for d in range(2):
        h, f = rglru_direction(xc, w_a[d], b_a[d], w_x[d], b_x[d], lam[d], h0[d], d == 1)
        hs.append(h)
        fins.append(f)
    return jax.nn.gelu(ga) * (hs[0] + hs[1]), jnp.stack(fins)


def token_shift(z, mu):
    zp = jnp.pad(z, ((0, 0), (1, 1), (0, 0)))
    return z + mu * (0.5 * (zp[:, :-2] + zp[:, 2:]) - z)


def wkv7_scan(r, w, k, v, kk, iclr, s0, reverse):
    def step(s, inp):
        r_t, w_t, k_t, v_t, kk_t, a_t = inp
        sa = jnp.einsum('bhvk,bhk->bhv', s, kk_t)
        s = (s * w_t[:, :, None, :] - sa[..., None] * (kk_t * a_t)[:, :, None, :]
             + v_t[..., None] * k_t[:, :, None, :])
        return s, jnp.einsum('bhvk,bhk->bhv', s, r_t)
    xs = tuple(jnp.moveaxis(t, 1, 0) for t in (r, w, k, v, kk, iclr))
    s_fin, y = lax.scan(step, s0, xs, reverse=reverse)
    return jnp.moveaxis(y, 0, 1), s_fin


def mixer_rwkv7(zb, mu, w0, w2, a0, a2, g2, k_k, k_a, r_k, ln_w, ln_b, s0):
    b, l, _ = zb.shape
    zb = token_shift(zb, mu)
    wdt = RWKV_WIDTH
    r, k, v, wd, ad, gd = jnp.split(
        zb, [wdt, 2 * wdt, 3 * wdt, 3 * wdt + 2 * DECAY_LORA, 3 * wdt + 2 * DECAY_LORA + 2 * ICLR_LORA], axis=-1)

    def heads(t):
        return t.reshape(b, l, RWKV_HEADS, RWKV_HEAD_DIM)

    kk = heads(k * k_k)
    kk = kk * lax.rsqrt(jnp.sum(kk * kk, axis=-1, keepdims=True) + 1e-12)
    r_h, v_h = heads(r), heads(v)
    wd = wd.reshape(b, l, 2, DECAY_LORA)
    ad = ad.reshape(b, l, 2, ICLR_LORA)
    ys, bonuses, fins = [], [], []
    for d in range(2):
        w_log = -jax.nn.softplus(-(w0[d] + jnp.tanh(wd[:, :, d]) @ w2[d])) - 0.5
        decay = jnp.exp(-jnp.exp(w_log))
        iclr = jax.nn.sigmoid(a0[d] + ad[:, :, d] @ a2[d])
        k_d = heads(k * (1.0 + (iclr - 1.0) * k_a))
        y_d, s_d = wkv7_scan(r_h, heads(decay), k_d, v_h, kk, heads(iclr), s0[d], d == 1)
        ys.append(y_d)
        bonuses.append(jnp.sum(r_h * k_d * r_k, axis=-1, keepdims=True) * v_h)
        fins.append(s_d)
    y = ys[0] + ys[1]
    mean = jnp.mean(y, axis=-1, keepdims=True)
    var = jnp.mean(jnp.square(y - mean), axis=-1, keepdims=True)
    y = ((y - mean) * lax.rsqrt(var + RWKV_GN_EPS)).reshape(b, l, wdt) * ln_w + ln_b
    y = y + (bonuses[0] + bonuses[1]).reshape(b, l, wdt)
    g = jax.nn.sigmoid(gd) @ g2
    return y * g, jnp.stack(fins)


def s5_direction(u, lam_re, lam_im, log_step, b_re, b_im, c_re, c_im, h0_re, h0_im, reverse):
    step = jnp.exp(log_step)[:, None]
    x_re, ang = lam_re * step, lam_im * step
    mag = jnp.exp(x_re)
    lb_re, lb_im = mag * jnp.cos(ang), mag * jnp.sin(ang)
    nr = jnp.expm1(x_re) * jnp.cos(ang) - 2.0 * jnp.square(jnp.sin(0.5 * ang))
    den = lam_re * lam_re + lam_im * lam_im
    f_re = (nr * lam_re + lb_im * lam_im) / den
    f_im = (lb_im * lam_re - nr * lam_im) / den
    bb_re = f_re[..., None] * b_re - f_im[..., None] * b_im
    bb_im = f_re[..., None] * b_im + f_im[..., None] * b_re
    bu_re = jnp.einsum('blgc,gpc->blgp', u, bb_re)
    bu_im = jnp.einsum('blgc,gpc->blgp', u, bb_im)
    a_re = jnp.broadcast_to(lb_re, bu_re.shape)
    a_im = jnp.broadcast_to(lb_im, bu_im.shape)
    h_re, h_im, fin_re, fin_im = complex_linear_scan(a_re, a_im, bu_re, bu_im, h0_re, h0_im, reverse)
    y = jnp.einsum('blgp,gcp->blgc', h_re, c_re) - jnp.einsum('blgp,gcp->blgc', h_im, c_im)
    return y, jnp.stack([fin_re, fin_im])


def mixer_s5(u, lam_re, lam_im, log_step, b_re, b_im, c_re, c_im, d_skip, w_glu, b_glu, h0):
    b, l, _ = u.shape
    ug = u.reshape(b, l, S5_GROUPS, S5_GROUP)
    ys, fins = [], []
    for d in range(2):
        y_d, f_d = s5_direction(ug, lam_re[d], lam_im[d], log_step[d], b_re[d], b_im[d],
                                c_re[d], c_im[d], h0[d, 0], h0[d, 1], d == 1)
        ys.append(y_d)
        fins.append(f_d)
    y = jax.nn.gelu((ys[0] + ys[1]).reshape(b, l, S5_WIDTH) + d_skip * u)
    return y * jax.nn.sigmoid(y @ w_glu + b_glu), jnp.stack(fins)


def token_mixers(z, lp, h_lru, s_rwkv, h_s5, rows):
    z = z.astype(jnp.float32)
    o1 = LRU_WIDTH
    o2 = 2 * LRU_WIDTH
    o3 = o2 + RWKV_IN
    o4 = o3 + S5_WIDTH
    xa, ga, zb, uc, zg = jnp.split(z, [o1, o2, o3, o4], axis=-1)
    y_a, fin_lru = mixer_rglru(xa, ga, lp['lru_conv_w'], lp['lru_conv_b'], lp['lru_wa'], lp['lru_ba'],
                               lp['lru_wx'], lp['lru_bx'], lp['lru_lam'], h_lru)
    y_b, fin_rwkv = mixer_rwkv7(zb, lp['rwkv_mu'], lp['rwkv_w0'], lp['rwkv_w2'], lp['rwkv_a0'], lp['rwkv_a2'],
                                lp['rwkv_g2'], lp['rwkv_kk'], lp['rwkv_ka'], lp['rwkv_rk'],
                                lp['rwkv_lnw'], lp['rwkv_lnb'], s_rwkv)
    if rows is not None:
        uc = to_col_major(uc, rows)
    y_c, fin_s5 = mixer_s5(uc, lp['s5_lam_re'], lp['s5_lam_im'], lp['s5_log_step'], lp['s5_b_re'],
                           lp['s5_b_im'], lp['s5_c_re'], lp['s5_c_im'], lp['s5_d'], lp['s5_w_glu'],
                           lp['s5_b_glu'], h_s5)
    if rows is not None:
        y_c = to_raster(y_c, rows)
    ys = jnp.stack([y_a, y_b, y_c], axis=2)
    return ys, zg, (fin_lru, fin_rwkv, fin_s5)


def merge_project(ys, zg, w_branch, w_out):
    b, l = ys.shape[:2]
    g = jax.nn.sigmoid(zg).reshape(b, l, N_BRANCH, D_MODEL)
    proj = jnp.einsum('blkw,kwd->blkd', ys, w_branch)
    return jnp.sum(g * proj, axis=2) @ w_out


def swiglu(xn, w_in, w_out):
    gate, up = jnp.split(xn @ w_in, 2, axis=-1)
    return (jax.nn.silu(gate) * up) @ w_out


def setup_inputs(seed: int = 0) -> dict:
    key = jax.random.key(seed)
    keys = iter(jax.random.split(key, 64))
    f32 = jnp.float32

    def nrm(shape, scale):
        return jax.random.normal(next(keys), shape, f32) * scale

    def gain(shape):
        return 1.0 + nrm(shape, 0.02)

    s = jax.random.uniform(next(keys), (DEPTH, 2, LRU_WIDTH), f32, 0.9, 0.999) ** (1.0 / LRU_C)
    lru_lam = jnp.log(s) - jnp.log1p(-s)
    rwkv_w0 = jnp.linspace(-6.0, -1.0, RWKV_WIDTH, dtype=f32) + nrm((DEPTH, 2, RWKV_WIDTH), 0.1)
    s5_shape = (DEPTH, 2, S5_GROUPS, S5_STATE)
    s5_lam_re = -0.5 + nrm(s5_shape, 0.01)
    s5_lam_im = math.pi * jnp.arange(S5_STATE, dtype=f32) + nrm(s5_shape, 0.01)
    s5_log_step = jax.random.uniform(next(keys), (DEPTH, 2, S5_GROUPS), f32, math.log(1e-3), math.log(1e-1))
    blk = (DEPTH, 2, LRU_HEADS, LRU_HEAD_DIM, LRU_HEAD_DIM)
    return {
        'x': nrm((BATCH, SEQ, D_MODEL), 1.0),
        'c': nrm((BATCH, D_MODEL), 1.0),
        'ctx': nrm((BATCH, CTX_LEN, D_MODEL), 1.0),
        'c_ctx': nrm((D_MODEL,), 1.0),
        'w_mod': nrm((DEPTH, D_MODEL, 6 * D_MODEL), 0.5 * D_MODEL ** -0.5),
        'b_mod': nrm((DEPTH, 6 * D_MODEL), 0.02),
        'norm1': gain((DEPTH, D_MODEL)),
        'norm2': gain((DEPTH, D_MODEL)),
        'norm_f': gain((D_MODEL,)),
        'w_in': nrm((DEPTH, D_MODEL, N_IN), D_MODEL ** -0.5),
        'lru_conv_w': nrm((DEPTH, CONV_W, LRU_WIDTH), CONV_W ** -0.5),
        'lru_conv_b': nrm((DEPTH, LRU_WIDTH), 0.02),
        'lru_wa': nrm(blk, LRU_HEAD_DIM ** -0.5),
        'lru_ba': nrm((DEPTH, 2, LRU_WIDTH), 0.02),
        'lru_wx': nrm(blk, LRU_HEAD_DIM ** -0.5),
        'lru_bx': nrm((DEPTH, 2, LRU_WIDTH), 0.02),
        'lru_lam': lru_lam,
        'rwkv_mu': jax.random.uniform(next(keys), (DEPTH, RWKV_IN), f32),
        'rwkv_w0': rwkv_w0,
        'rwkv_w2': nrm((DEPTH, 2, DECAY_LORA, RWKV_WIDTH), 0.1),
        'rwkv_a0': nrm((DEPTH, 2, RWKV_WIDTH), 0.1),
        'rwkv_a2': nrm((DEPTH, 2, ICLR_LORA, RWKV_WIDTH), 0.1),
        'rwkv_g2': nrm((DEPTH, GATE_LORA, RWKV_WIDTH), GATE_LORA ** -0.5),
        'rwkv_kk': 0.85 + nrm((DEPTH, RWKV_WIDTH), 0.02),
        'rwkv_ka': gain((DEPTH, RWKV_WIDTH)),
        'rwkv_rk': nrm((DEPTH, RWKV_HEADS, RWKV_HEAD_DIM), 0.1),
        'rwkv_lnw': gain((DEPTH, RWKV_WIDTH)),
        'rwkv_lnb': nrm((DEPTH, RWKV_WIDTH), 0.02),
        's5_lam_re': s5_lam_re,
        's5_lam_im': s5_lam_im,
        's5_log_step': s5_log_step,
        's5_b_re': nrm((DEPTH, 2, S5_GROUPS, S5_STATE, S5_GROUP), (2 * S5_GROUP) ** -0.5),
        's5_b_im': nrm((DEPTH, 2, S5_GROUPS, S5_STATE, S5_GROUP), (2 * S5_GROUP) ** -0.5),
        's5_c_re': nrm((DEPTH, 2, S5_GROUPS, S5_GROUP, S5_STATE), (2 * S5_STATE) ** -0.5),
        's5_c_im': nrm((DEPTH, 2, S5_GROUPS, S5_GROUP, S5_STATE), (2 * S5_STATE) ** -0.5),
        's5_d': nrm((DEPTH, S5_WIDTH), 1.0),
        's5_w_glu': nrm((DEPTH, S5_WIDTH, S5_WIDTH), S5_WIDTH ** -0.5),
        's5_b_glu': nrm((DEPTH, S5_WIDTH), 0.02),
        'w_branch': nrm((DEPTH, N_BRANCH, BRANCH_WIDTH, D_MODEL), BRANCH_WIDTH ** -0.5),
        'w_out': nrm((DEPTH, D_MODEL, D_MODEL), D_MODEL ** -0.5),
        'w_ffn_in': nrm((DEPTH, D_MODEL, 2 * FFN_HIDDEN), D_MODEL ** -0.5),
        'w_ffn_out': nrm((DEPTH, FFN_HIDDEN, D_MODEL), FFN_HIDDEN ** -0.5),
    }


def reference(x, c, ctx, c_ctx, w_mod, b_mod, norm1, norm2, norm_f, w_in,
              lru_conv_w, lru_conv_b, lru_wa, lru_ba, lru_wx, lru_bx, lru_lam,
              rwkv_mu, rwkv_w0, rwkv_w2, rwkv_a0, rwkv_a2, rwkv_g2, rwkv_kk, rwkv_ka, rwkv_rk,
              rwkv_lnw, rwkv_lnb,
              s5_lam_re, s5_lam_im, s5_log_step, s5_b_re, s5_b_im, s5_c_re, s5_c_im, s5_d,
              s5_w_glu, s5_b_glu,
              w_branch, w_out, w_ffn_in, w_ffn_out):
    f32 = jnp.float32
    bsz, n_tok, _ = x.shape
    rows = n_tok // GRID_W
    h_lat, h_ctx = x, ctx
    cond_lat = jax.nn.silu(c)
    cond_ctx = jax.nn.silu(c_ctx)
    zero_lru = jnp.zeros((2, bsz, LRU_WIDTH), f32)
    zero_rwkv = jnp.zeros((2, bsz, RWKV_HEADS, RWKV_HEAD_DIM, RWKV_HEAD_DIM), f32)
    zero_s5 = jnp.zeros((2, 2, bsz, S5_GROUPS, S5_STATE), f32)
    for i in range(DEPTH):
        last = i == DEPTH - 1
        m_lat = (cond_lat @ w_mod[i] + b_mod[i])[:, None, :]
        m_ctx = cond_ctx @ w_mod[i] + b_mod[i]
        sh1l, sc1l, gt1l, sh2l, sc2l, gt2l = jnp.split(m_lat, 6, axis=-1)
        sh1c, sc1c, gt1c, sh2c, sc2c, gt2c = jnp.split(m_ctx, 6, axis=-1)
        lp = dict(lru_conv_w=lru_conv_w[i], lru_conv_b=lru_conv_b[i], lru_wa=lru_wa[i], lru_ba=lru_ba[i],
                  lru_wx=lru_wx[i], lru_bx=lru_bx[i], lru_lam=lru_lam[i],
                  rwkv_mu=rwkv_mu[i], rwkv_w0=rwkv_w0[i], rwkv_w2=rwkv_w2[i], rwkv_a0=rwkv_a0[i],
                  rwkv_a2=rwkv_a2[i], rwkv_g2=rwkv_g2[i], rwkv_kk=rwkv_kk[i], rwkv_ka=rwkv_ka[i],
                  rwkv_rk=rwkv_rk[i], rwkv_lnw=rwkv_lnw[i], rwkv_lnb=rwkv_lnb[i],
                  s5_lam_re=s5_lam_re[i], s5_lam_im=s5_lam_im[i], s5_log_step=s5_log_step[i],
                  s5_b_re=s5_b_re[i], s5_b_im=s5_b_im[i], s5_c_re=s5_c_re[i], s5_c_im=s5_c_im[i],
                  s5_d=s5_d[i], s5_w_glu=s5_w_glu[i], s5_b_glu=s5_b_glu[i])
        z_ctx = modulate(rms_norm(h_ctx, norm1[i]), sh1c, sc1c) @ w_in[i]
        z_lat = modulate(rms_norm(h_lat, norm1[i]), sh1l, sc1l) @ w_in[i]
        ys_c, zg_c, (st_lru, st_rwkv, st_s5) = token_mixers(z_ctx, lp, zero_lru, zero_rwkv, zero_s5, None)
        ys_l, zg_l, _ = token_mixers(z_lat, lp, st_lru, st_rwkv, st_s5, rows)
        h_lat = h_lat + (gt1l * merge_project(ys_l, zg_l, w_branch[i], w_out[i])).astype(h_lat.dtype)
        h_lat = h_lat + (gt2l * swiglu(modulate(rms_norm(h_lat, norm2[i]), sh2l, sc2l),
                                       w_ffn_in[i], w_ffn_out[i])).astype(h_lat.dtype)
        if not last:
            h_ctx = h_ctx + (gt1c * merge_project(ys_c, zg_c, w_branch[i], w_out[i])).astype(h_ctx.dtype)
            h_ctx = h_ctx + (gt2c * swiglu(modulate(rms_norm(h_ctx, norm2[i]), sh2c, sc2c),
                                           w_ffn_in[i], w_ffn_out[i])).astype(h_ctx.dtype)
    return rms_norm(h_lat, norm_f)
```

```python
import functools
import math

import jax
import jax.numpy as jnp
from jax import lax
from jax.experimental import pallas as pl
from jax.experimental.pallas import tpu as pltpu

F32 = jnp.float32
BF16 = jnp.bfloat16
HIGHEST = lax.Precision.HIGHEST

D = 1024
WIDTH = 512
GRID_W = 64
EPS = 1e-6
LRU_C = 8.0
HEAD = 64
RWKV_IN = 3 * WIDTH + 2 * 64 + 2 * 64 + 128
GN_EPS = 64e-5
S5_GROUPS = 32
S5_GROUP = 16
S5_STATE = 64
S5_CHUNK = 16
N_A = 2 * WIDTH + RWKV_IN + WIDTH
FFN_HIDDEN = 2816
WKV_CHUNK = 64
LANE_TILE = 128
SUBLANE_TILE = 8
QUAD = 4 * HEAD
VMEM_LIMIT = 56 * 1024 * 1024


def _bf(x):
    return x.astype(BF16)


def _mm(a, b):
    return jnp.dot(_bf(a), _bf(b), preferred_element_type=F32)


def _mm_nt(a, b):
    return lax.dot_general(_bf(a), _bf(b), (((1,), (1,)), ((), ())), preferred_element_type=F32)


def _mm_tn(a, b):
    return lax.dot_general(_bf(a), _bf(b), (((0,), (0,)), ((), ())), preferred_element_type=F32)


def _softplus(x):
    return jnp.maximum(x, 0.0) + jnp.log(1.0 + jnp.exp(-jnp.abs(x)))


def _head_sum(x, bd_ones):
    hi = _bf(x)
    lo = _bf(x - hi.astype(F32))
    return (jnp.dot(hi, bd_ones, preferred_element_type=F32)
            + jnp.dot(lo, bd_ones, preferred_element_type=F32))


def _mod_norm(x, g, shift, scale):
    y = x * lax.rsqrt(jnp.mean(x * x, axis=-1, keepdims=True) + EPS) * g
    return y * (1.0 + scale) + shift


def _params(n_axes=1):
    return pltpu.CompilerParams(dimension_semantics=("arbitrary",) * n_axes,
                                vmem_limit_bytes=VMEM_LIMIT)


def _layer_spec(arr, layer):
    rest = arr.shape[1:]
    return pl.BlockSpec((None,) + rest, lambda *_: (layer,) + (0,) * len(rest))


def _full_spec(arr):
    return pl.BlockSpec(arr.shape, lambda *_: (0,) * arr.ndim)


def _row_spec(tm, width, col=0):
    return pl.BlockSpec((tm, width), lambda i: (i, col))


def _mod_kernel(cond_ref, w_ref, b_ref, o_ref):
    cnd = cond_ref[...]
    act = cnd * jax.nn.sigmoid(cnd)
    o_ref[...] = jnp.dot(act, w_ref[...], preferred_element_type=F32, precision=HIGHEST) + b_ref[...]


def _modulation(cond, w_mod, b_mod):
    depth = w_mod.shape[0]
    n_col = w_mod.shape[2] // D
    return pl.pallas_call(
        _mod_kernel,
        grid=(depth, n_col),
        in_specs=[pl.BlockSpec((SUBLANE_TILE, D), lambda l, j: (0, 0)),
                  pl.BlockSpec((None, D, D), lambda l, j: (l, 0, j)),
                  pl.BlockSpec((None, 1, D), lambda l, j: (l, 0, j))],
        out_specs=pl.BlockSpec((None, SUBLANE_TILE, D), lambda l, j: (l, 0, j)),
        out_shape=jax.ShapeDtypeStruct((depth, SUBLANE_TILE, 6 * D), F32),
        compiler_params=_params(2),
        name="modulation",
    )(cond, w_mod, b_mod.reshape(depth, 1, 6 * D))


def _premix_kernel(h_ref, hp_ref, hn_ref, mod_ref, g_ref, w_ref, cw_ref, cb_ref, mu_ref, kk_ref, bd_ref,
                   xc_ref, ga_ref, zb_ref, uc_ref, kkn_ref, z_scr, *, row, tm):
    i = pl.program_id(0)
    n = pl.num_programs(0)
    halo = SUBLANE_TILE
    shift = mod_ref[row:row + 1, 0:D]
    scale = mod_ref[row:row + 1, D:2 * D]
    hext = jnp.concatenate([hp_ref[...], h_ref[...], hn_ref[...]], axis=0)
    xn = _mod_norm(hext, g_ref[...], shift, scale)
    z_scr[...] = _mm(xn, w_ref[...])

    @pl.when(i == 0)
    def _():
        z_scr[0:halo, :] = jnp.zeros((halo, N_A), F32)

    @pl.when(i == n - 1)
    def _():
        z_scr[tm + halo:tm + 2 * halo, :] = jnp.zeros((halo, N_A), F32)

    acc = cb_ref[...] + cw_ref[0:1, :] * z_scr[halo - 2:halo - 2 + tm, 0:WIDTH]
    for j in range(1, 4):
        acc = acc + cw_ref[j:j + 1, :] * z_scr[halo - 2 + j:halo - 2 + j + tm, 0:WIDTH]
    xc_ref[...] = acc
    ga_ref[...] = z_scr[halo:halo + tm, WIDTH:2 * WIDTH]
    lo, hi = 2 * WIDTH, 2 * WIDTH + RWKV_IN
    zc = z_scr[halo:halo + tm, lo:hi]
    zp = z_scr[halo - 1:halo - 1 + tm, lo:hi]
    zn = z_scr[halo + 1:halo + 1 + tm, lo:hi]
    zs = zc + mu_ref[...] * (0.5 * (zp + zn) - zc)
    zb_ref[...] = zs
    uc_ref[...] = z_scr[halo:halo + tm, hi:hi + WIDTH]
    kk = zs[:, WIDTH:2 * WIDTH] * kk_ref[...]
    ss = _head_sum(kk * kk, bd_ref[...])
    kkn_ref[...] = kk * lax.rsqrt(ss + 1e-12)


def _premix(h, layer, row, mod, norm1, w_a, conv_w, conv_b, mu, kk_w, bd_ones):
    n_tok = h.shape[0]
    tm = min(256, n_tok)
    n = n_tok // tm
    per = tm // SUBLANE_TILE
    last_blk = n_tok // SUBLANE_TILE - 1
    kern = functools.partial(_premix_kernel, row=row, tm=tm)
    outs = pl.pallas_call(
        kern,
        grid=(n,),
        in_specs=[
            _row_spec(tm, D),
            pl.BlockSpec((SUBLANE_TILE, D), lambda i: (jnp.maximum(i * per - 1, 0), 0)),
            pl.BlockSpec((SUBLANE_TILE, D), lambda i: (jnp.minimum((i + 1) * per, last_blk), 0)),
            _layer_spec(mod, layer), _layer_spec(norm1, layer), _layer_spec(w_a, layer),
            _layer_spec(conv_w, layer), _layer_spec(conv_b, layer), _layer_spec(mu, layer),
            _layer_spec(kk_w, layer), _full_spec(bd_ones),
        ],
        out_specs=[_row_spec(tm, WIDTH), _row_spec(tm, WIDTH), _row_spec(tm, RWKV_IN),
                   _row_spec(tm, WIDTH), _row_spec(tm, WIDTH)],
        out_shape=[jax.ShapeDtypeStruct((n_tok, WIDTH), F32), jax.ShapeDtypeStruct((n_tok, WIDTH), F32),
                   jax.ShapeDtypeStruct((n_tok, RWKV_IN), F32), jax.ShapeDtypeStruct((n_tok, WIDTH), F32),
                   jax.ShapeDtypeStruct((n_tok, WIDTH), F32)],
        scratch_shapes=[pltpu.VMEM((tm + 2 * SUBLANE_TILE, N_A), F32)],
        compiler_params=_params(),
        name="premix",
    )(h, h, h, mod, norm1, w_a, conv_w, conv_b, mu, kk_w, bd_ones)
    return outs


def _lru_kernel(xf_ref, xr_ref, wg_ref, bg_ref, lam_ref, h0_ref, hf_ref, hr_ref, fin_ref, carry, *, tm):
    i = pl.program_id(0)

    @pl.when(i == 0)
    def _():
        carry[...] = h0_ref[...]

    rows = lax.broadcasted_iota(jnp.int32, (tm, WIDTH), 0)
    for d, (x_ref, o_ref) in enumerate(((xf_ref, hf_ref), (xr_ref, hr_ref))):
        xc = x_ref[...]
        gates = _mm(xc, wg_ref[d]) + bg_ref[d]
        gate_r = jax.nn.sigmoid(gates[:, 0:WIDTH])
        gate_i = jax.nn.sigmoid(gates[:, WIDTH:2 * WIDTH])
        log_a = -LRU_C * gate_r * _softplus(-lam_ref[d])
        a = jnp.exp(log_a)
        b = jnp.sqrt(1.0 - jnp.exp(2.0 * log_a)) * (gate_i * xc)
        s = 1
        while s < tm:
            if d == 0:
                a_sh, b_sh, m = pltpu.roll(a, s, 0), pltpu.roll(b, s, 0), rows >= s
            else:
                a_sh, b_sh, m = pltpu.roll(a, tm - s, 0), pltpu.roll(b, tm - s, 0), rows < tm - s
            b = jnp.where(m, a * b_sh + b, b)
            a = jnp.where(m, a * a_sh, a)
            s *= 2
        hs = a * carry[d] + b
        o_ref[...] = hs
        carry[d] = hs[tm - 1:tm, :] if d == 0 else hs[0:1, :]
    fin_ref[...] = carry[...]


def _lru(xc, layer, w_gate, b_gate, lam, h0):
    n_tok = xc.shape[0]
    tm = min(256, n_tok)
    n = n_tok // tm
    kern = functools.partial(_lru_kernel, tm=tm)
    return pl.pallas_call(
        kern,
        grid=(n,),
        in_specs=[_row_spec(tm, WIDTH),
                  pl.BlockSpec((tm, WIDTH), lambda i: (n - 1 - i, 0)),
                  _layer_spec(w_gate, layer), _layer_spec(b_gate, layer), _layer_spec(lam, layer),
                  _full_spec(h0)],
        out_specs=[_row_spec(tm, WIDTH),
                   pl.BlockSpec((tm, WIDTH), lambda i: (n - 1 - i, 0)),
                   pl.BlockSpec((2, 1, WIDTH), lambda i: (0, 0, 0))],
        out_shape=[jax.ShapeDtypeStruct((n_tok, WIDTH), F32), jax.ShapeDtypeStruct((n_tok, WIDTH), F32),
                   jax.ShapeDtypeStruct((2, 1, WIDTH), F32)],
        scratch_shapes=[pltpu.VMEM((2, 1, WIDTH), F32)],
        compiler_params=_params(),
        name="rglru_scan",
    )(xc, xc, w_gate, b_gate, lam, h0)


def _block_rows(x, lane_head):
    return jnp.concatenate([jnp.where(lane_head == h, x, 0.0) for h in range(QUAD // HEAD)], axis=0)


def _wkv_kernel(zf_ref, zr_ref, kf_ref, kr_ref, w2_ref, a2_ref, w0_ref, a0_ref, ka_ref, rk_ref, bd_ref, s0_ref,
                yf_ref, yr_ref, bf_ref, br_ref, sfin_ref, s_scr):
    i = pl.program_id(0)
    t = WKV_CHUNK

    @pl.when(i == 0)
    def _():
        s_scr[...] = s0_ref[...]

    row = lax.broadcasted_iota(jnp.int32, (t, QUAD), 0)
    lane = lax.broadcasted_iota(jnp.int32, (t, QUAD), 1)
    lane_head = lane >> 6
    lane_tok = lane & (HEAD - 1)
    row_t = lax.broadcasted_iota(jnp.int32, (t, t), 0)
    col_t = lax.broadcasted_iota(jnp.int32, (t, t), 1)
    same_head = (lax.broadcasted_iota(jnp.int32, (QUAD, QUAD), 0) >> 6) == (
        lax.broadcasted_iota(jnp.int32, (QUAD, QUAD), 1) >> 6)
    eye = jnp.where(lane_tok == row, 1.0, 0.0)
    pair_mask = [((row ^ lane_tok) >> lvl) == 1 for lvl in range(int(math.log2(t)))]

    dirs = ((zf_ref, kf_ref, yf_ref, bf_ref), (zr_ref, kr_ref, yr_ref, br_ref))
    for d, (z_ref, k_ref, y_ref, b_ref) in enumerate(dirs):
        zb = z_ref[...]
        r = zb[:, 0:WIDTH]
        k = zb[:, WIDTH:2 * WIDTH]
        v = zb[:, 2 * WIDTH:3 * WIDTH]
        wd = zb[:, 3 * WIDTH:3 * WIDTH + LANE_TILE]
        ad = zb[:, 3 * WIDTH + LANE_TILE:3 * WIDTH + 2 * LANE_TILE]
        kkn = k_ref[...]
        w_log = -_softplus(-(w0_ref[d] + _mm(jnp.tanh(wd), w2_ref[d]))) - 0.5
        lw = -jnp.exp(w_log)
        iclr = jax.nn.sigmoid(a0_ref[d] + _mm(ad, a2_ref[d]))
        kd = k * (1.0 + (iclr - 1.0) * ka_ref[...])
        b_ref[...] = _head_sum(r * kd * rk_ref[...], bd_ref[...]) * v

        tri = jnp.where(col_t <= row_t if d == 0 else col_t >= row_t, 1.0, 0.0).astype(BF16)
        p1 = _bf(lw)
        r1 = lw - p1.astype(F32)
        p2 = _bf(r1)
        p3 = _bf(r1 - p2.astype(F32))
        cum3 = jnp.dot(tri, jnp.concatenate([p1, p2, p3], axis=1), preferred_element_type=F32)
        cum = cum3[:, 0:WIDTH] + cum3[:, WIDTH:2 * WIDTH] + cum3[:, 2 * WIDTH:3 * WIDTH]
        tot = cum[t - 1:t, :] if d == 0 else cum[0:1, :]
        e_neg = jnp.exp(-cum)
        e_end = jnp.exp(tot - cum)
        kb = kkn * iclr
        a_t = -kkn * jnp.exp(cum - lw)
        r_t = r * jnp.exp(cum)
        b_t = kb * e_neg
        k_t = kd * e_neg
        b_end = kb * e_end
        k_end = kd * e_end
        decay_tot = jnp.exp(tot)

        if d == 0:
            strict, incl = lane_tok < row, lane_tok <= row
        else:
            strict, incl = lane_tok > row, lane_tok >= row

        ys = []
        for q in range(WIDTH // QUAD):
            sl = slice(q * QUAD, (q + 1) * QUAD)
            at, rt, vq = a_t[:, sl], r_t[:, sl], v[:, sl]
            ar = jnp.concatenate([at, rt], axis=0)
            gb = _mm_nt(ar, _block_rows(b_t[:, sl], lane_head))
            gk = _mm_nt(ar, _block_rows(k_t[:, sl], lane_head))
            a_ab = jnp.where(strict, gb[0:t], 0.0)
            a_rb = jnp.where(incl, gb[t:2 * t], 0.0)
            a_ak = jnp.where(strict, gk[0:t], 0.0)
            a_rk = jnp.where(incl, gk[t:2 * t], 0.0)
            inv = eye + jnp.where(pair_mask[0], a_ab, 0.0)
            for lvl in range(1, len(pair_mask)):
                off = jnp.where(pair_mask[lvl], a_ab, 0.0)
                half = _mm(inv, _block_rows(off, lane_head))
                inv = inv + _mm(half, _block_rows(inv, lane_head))
            s_old = s_scr[d, q]
            v_bd = _block_rows(vq, lane_head)
            u = _mm(inv, _block_rows(_mm_nt(at, s_old) + _mm(a_ak, v_bd), lane_head))
            ys.append(_mm_nt(rt, s_old) + _mm(a_rb, _block_rows(u, lane_head)) + _mm(a_rk, v_bd))
            upd = _mm_tn(jnp.concatenate([u, vq], axis=0),
                         jnp.concatenate([b_end[:, sl], k_end[:, sl]], axis=0))
            s_scr[d, q] = s_old * decay_tot[:, sl] + jnp.where(same_head, upd, 0.0)
        y_ref[...] = jnp.concatenate(ys, axis=1)
    sfin_ref[...] = s_scr[...]


def _wkv(zbs, kkn, layer, w2p, a2p, w0, a0, ka, rk, bd_ones, s0):
    n_tok = zbs.shape[0]
    t = WKV_CHUNK
    n = n_tok // t
    fwd = lambda i: (i, 0)
    rev = lambda i: (n - 1 - i, 0)
    state_shape = s0.shape
    return pl.pallas_call(
        _wkv_kernel,
        grid=(n,),
        in_specs=[pl.BlockSpec((t, RWKV_IN), fwd), pl.BlockSpec((t, RWKV_IN), rev),
                  pl.BlockSpec((t, WIDTH), fwd), pl.BlockSpec((t, WIDTH), rev),
                  _layer_spec(w2p, layer), _layer_spec(a2p, layer), _layer_spec(w0, layer),
                  _layer_spec(a0, layer), _layer_spec(ka, layer), _layer_spec(rk, layer),
                  _full_spec(bd_ones), _full_spec(s0)],
        out_specs=[pl.BlockSpec((t, WIDTH), fwd), pl.BlockSpec((t, WIDTH), rev),
                   pl.BlockSpec((t, WIDTH), fwd), pl.BlockSpec((t, WIDTH), rev),
                   pl.BlockSpec(state_shape, lambda i: (0, 0, 0, 0))],
        out_shape=[jax.ShapeDtypeStruct((n_tok, WIDTH), F32)] * 4 + [jax.ShapeDtypeStruct(state_shape, F32)],
        scratch_shapes=[pltpu.VMEM(state_shape, F32)],
        compiler_params=_params(),
        name="wkv7_scan",
    )(zbs, zbs, kkn, kkn, w2p, a2p, w0, a0, ka, rk, bd_ones, s0)


def _cmul_rows(x, p):
    half = LANE_TILE // 2
    lane = lax.broadcasted_iota(jnp.int32, p.shape, 1)
    p_sw = pltpu.roll(p, half, 1)
    p1 = jnp.where(lane < half, p, p_sw)[0:1, :]
    p2 = jnp.where(lane < half, -p_sw, p)[0:1, :]
    return x * p1 + pltpu.roll(x, half, 1) * p2


def _s5_kernel(u_ref, w_ref, v_ref, lam_ref, h0_ref, y_ref, fin_ref, *, nc):
    blk = S5_CHUNK * S5_GROUP
    res = _mm(u_ref[...], w_ref[...])
    row = lax.broadcasted_iota(jnp.int32, (nc, LANE_TILE), 0)
    h_in = []
    for d in range(2):
        e = res[:, blk + d * LANE_TILE:blk + (d + 1) * LANE_TILE]
        lam = jnp.broadcast_to(lam_ref[d], (SUBLANE_TILE, LANE_TILE))
        h0 = jnp.broadcast_to(h0_ref[d], (SUBLANE_TILE, LANE_TILE))
        first = 0 if d == 0 else nc - 1
        e = jnp.where(row == first, e + _cmul_rows(h0, lam)[0:1, :], e)
        p = lam
        s = 1
        while s < nc:
            if d == 0:
                sh, m = pltpu.roll(e, s, 0), row >= s
            else:
                sh, m = pltpu.roll(e, nc - s, 0), row < nc - s
            e = e + jnp.where(m, _cmul_rows(sh, p), 0.0)
            p = _cmul_rows(p, p)
            s *= 2
        if d == 0:
            fin_ref[d] = e[nc - 1:nc, :]
            h_in.append(jnp.where(row >= 1, pltpu.roll(e, 1, 0), h0[0:1, :]))
        else:
            fin_ref[d] = e[0:1, :]
            h_in.append(jnp.where(row < nc - 1, pltpu.roll(e, nc - 1, 0), h0[0:1, :]))
    y_ref[...] = res[:, 0:blk] + _mm(jnp.concatenate(h_in, axis=1), v_ref[...])


def _s5(u_blocks, layer, w_cat, v_cat, lam16, h0):
    groups, nc, blk = u_blocks.shape
    kern = functools.partial(_s5_kernel, nc=nc)
    return pl.pallas_call(
        kern,
        grid=(groups,),
        in_specs=[pl.BlockSpec((None, nc, blk), lambda g: (g, 0, 0)),
                  pl.BlockSpec((None, None, blk, w_cat.shape[-1]), lambda g: (layer, g, 0, 0)),
                  pl.BlockSpec((None, None, blk, blk), lambda g: (layer, g, 0, 0)),
                  pl.BlockSpec((None, 2, None, 1, LANE_TILE), lambda g: (layer, 0, g, 0, 0)),
                  pl.BlockSpec((2, None, 1, LANE_TILE), lambda g: (0, g, 0, 0))],
        out_specs=[pl.BlockSpec((None, nc, blk), lambda g: (g, 0, 0)),
                   pl.BlockSpec((2, None, 1, LANE_TILE), lambda g: (0, g, 0, 0))],
        out_shape=[jax.ShapeDtypeStruct((groups, nc, blk), F32),
                   jax.ShapeDtypeStruct((2, groups, 1, LANE_TILE), F32)],
        compiler_params=_params(),
        name="s5_scan",
    )(u_blocks, w_cat, v_cat, lam16, h0)


def _s5_weights(lam_re, lam_im, log_step, b_re, b_im, c_re, c_im):
    n = S5_CHUNK
    step = jnp.exp(log_step)[..., None]
    x_re, ang = lam_re * step, lam_im * step
    mag = jnp.exp(x_re)
    lb_re, lb_im = mag * jnp.cos(ang), mag * jnp.sin(ang)
    nr = jnp.expm1(x_re) * jnp.cos(ang) - 2.0 * jnp.square(jnp.sin(0.5 * ang))
    den = lam_re * lam_re + lam_im * lam_im
    f_re = (nr * lam_re + lb_im * lam_im) / den
    f_im = (lb_im * lam_re - nr * lam_im) / den
    bb_re = f_re[..., None] * b_re - f_im[..., None] * b_im
    bb_im = f_re[..., None] * b_im + f_im[..., None] * b_re
    pr, pi = [jnp.ones_like(lb_re)], [jnp.zeros_like(lb_re)]
    for _ in range(n):
        pr, pi = pr + [pr[-1] * lb_re - pi[-1] * lb_im], pi + [pr[-1] * lb_im + pi[-1] * lb_re]
    pw_re, pw_im = jnp.stack(pr, axis=-2), jnp.stack(pi, axis=-2)
    cp_re = c_re[..., None, :, :] * pw_re[..., :, None, :] - c_im[..., None, :, :] * pw_im[..., :, None, :]
    cp_im = c_re[..., None, :, :] * pw_im[..., :, None, :] + c_im[..., None, :, :] * pw_re[..., :, None, :]
    taps = (jnp.einsum('...jcp,...pk->...jck', cp_re, bb_re, precision=HIGHEST)
            - jnp.einsum('...jcp,...pk->...jck', cp_im, bb_im, precision=HIGHEST))
    s_idx = jnp.arange(n)[:, None]
    t_idx = jnp.arange(n)[None, :]

    def toeplitz(k, lag):
        m = jnp.take(k, jnp.clip(lag, 0, n).reshape(-1), axis=-3)
        m = m.reshape(k.shape[:-3] + (n, n, S5_GROUP, S5_GROUP))
        m = jnp.where((lag >= 0)[:, :, None, None], m, 0.0)
        m = jnp.swapaxes(jnp.swapaxes(m, -1, -2), -2, -3)
        return m.reshape(k.shape[:-3] + (n * S5_GROUP, n * S5_GROUP))

    m_tot = toeplitz(taps[:, 0], t_idx - s_idx) + toeplitz(taps[:, 1], s_idx - t_idx)

    def end_state(d, order):
        qr, qi = pw_re[:, d][..., order, :], pw_im[:, d][..., order, :]
        er = qr[..., :, None, :] * jnp.swapaxes(bb_re[:, d], -1, -2)[..., None, :, :] \
            - qi[..., :, None, :] * jnp.swapaxes(bb_im[:, d], -1, -2)[..., None, :, :]
        ei = qr[..., :, None, :] * jnp.swapaxes(bb_im[:, d], -1, -2)[..., None, :, :] \
            + qi[..., :, None, :] * jnp.swapaxes(bb_re[:, d], -1, -2)[..., None, :, :]
        e = jnp.concatenate([er, ei], axis=-1)
        return e.reshape(e.shape[:-3] + (n * S5_GROUP, 2 * S5_STATE))

    def state_out(d, order):
        vr = jnp.moveaxis(cp_re[:, d][..., order, :, :], -1, -3)
        vi = -jnp.moveaxis(cp_im[:, d][..., order, :, :], -1, -3)
        vv = jnp.concatenate([vr, vi], axis=-3)
        return vv.reshape(vv.shape[:-3] + (2 * S5_STATE, n * S5_GROUP))

    asc = jnp.arange(n)
    w_cat = jnp.concatenate([m_tot, end_state(0, n - 1 - asc), end_state(1, asc)], axis=-1)
    v_cat = jnp.concatenate([state_out(0, asc + 1), state_out(1, n - asc)], axis=-2)
    lam16 = jnp.concatenate([pw_re[..., n, :], pw_im[..., n, :]], axis=-1)[..., None, :]
    return _bf(w_cat), _bf(v_cat), lam16


def _merge_kernel(h_ref, hf_ref, hr_ref, ga_ref, yf_ref, yr_ref, bf_ref, br_ref, gd_ref, ys_ref, uc_ref,
                  mod_ref, g_ref, wzg_ref, lnw_ref, lnb_ref, g2_ref, dsk_ref, wglu_ref, bglu_ref,
                  wbr_ref, wout_ref, bd_ref, o_ref, *, row):
    h = h_ref[...]
    shift = mod_ref[row:row + 1, 0:D]
    scale = mod_ref[row:row + 1, D:2 * D]
    gate = mod_ref[row:row + 1, 2 * D:3 * D]
    zg = _mm(_mod_norm(h, g_ref[...], shift, scale), wzg_ref[...])
    bd = bd_ref[...]
    y_a = jax.nn.gelu(ga_ref[...]) * (hf_ref[...] + hr_ref[...])
    y = yf_ref[...] + yr_ref[...]
    inv_n = 1.0 / HEAD
    yc = y - _head_sum(y, bd) * inv_n
    var = _head_sum(yc * yc, bd) * inv_n
    y_b = yc * lax.rsqrt(var + GN_EPS) * lnw_ref[...] + lnb_ref[...] + (bf_ref[...] + br_ref[...])
    y_b = y_b * _mm(jax.nn.sigmoid(gd_ref[...]), g2_ref[...])
    y_s = jax.nn.gelu(ys_ref[...] + dsk_ref[...] * uc_ref[...])
    y_c = y_s * jax.nn.sigmoid(_mm(y_s, wglu_ref[...]) + bglu_ref[...])
    mix = (jax.nn.sigmoid(zg[:, 0:D]) * _mm(y_a, wbr_ref[0])
           + jax.nn.sigmoid(zg[:, D:2 * D]) * _mm(y_b, wbr_ref[1])
           + jax.nn.sigmoid(zg[:, 2 * D:3 * D]) * _mm(y_c, wbr_ref[2]))
    o_ref[...] = h + gate * _mm(mix, wout_ref[...])


def _merge(h, parts, layer, row, mod, norm1, w_zg, lnw, lnb, g2, dsk, w_glu, b_glu, w_branch, w_out, bd_ones):
    n_tok = h.shape[0]
    tm = min(256, n_tok)
    hf, hr, ga, yf, yr, bf_, br_, zbs, ys5, uc = parts
    kern = functools.partial(_merge_kernel, row=row)
    rs = _row_spec(tm, WIDTH)
    gd_col = (3 * WIDTH + 2 * LANE_TILE) // LANE_TILE
    weights = (mod, norm1, w_zg, lnw, lnb, g2, dsk, w_glu, b_glu, w_branch, w_out)
    return pl.pallas_call(
        kern,
        grid=(n_tok // tm,),
        in_specs=[_row_spec(tm, D), rs, rs, rs, rs, rs, rs, rs, _row_spec(tm, LANE_TILE, gd_col), rs, rs]
        + [_layer_spec(w, layer) for w in weights] + [_full_spec(bd_ones)],
        out_specs=_row_spec(tm, D),
        out_shape=jax.ShapeDtypeStruct((n_tok, D), F32),
        compiler_params=_params(),
        name="merge_project",
    )(h, hf, hr, ga, yf, yr, bf_, br_, zbs, ys5, uc, *weights, bd_ones)


def _ffn_kernel(h_ref, mod_ref, g_ref, win_ref, wout_ref, o_ref, *, row):
    h = h_ref[...]
    shift = mod_ref[row:row + 1, 3 * D:4 * D]
    scale = mod_ref[row:row + 1, 4 * D:5 * D]
    gate = mod_ref[row:row + 1, 5 * D:6 * D]
    gu = _mm(_mod_norm(h, g_ref[...], shift, scale), win_ref[...])
    a, up = gu[:, 0:FFN_HIDDEN], gu[:, FFN_HIDDEN:2 * FFN_HIDDEN]
    o_ref[...] = h + gate * _mm(a * jax.nn.sigmoid(a) * up, wout_ref[...])


def _ffn(h, layer, row, mod, norm2, w_in, w_out):
    n_tok = h.shape[0]
    tm = min(256, n_tok)
    kern = functools.partial(_ffn_kernel, row=row)
    return pl.pallas_call(
        kern,
        grid=(n_tok // tm,),
        in_specs=[_row_spec(tm, D), _layer_spec(mod, layer), _layer_spec(norm2, layer),
                  _layer_spec(w_in, layer), _layer_spec(w_out, layer)],
        out_specs=_row_spec(tm, D),
        out_shape=jax.ShapeDtypeStruct((n_tok, D), F32),
        compiler_params=_params(),
        name="swiglu",
    )(h, mod, norm2, w_in, w_out)


def _final_norm_kernel(h_ref, g_ref, o_ref):
    x = h_ref[...]
    o_ref[...] = x * lax.rsqrt(jnp.mean(x * x, axis=-1, keepdims=True) + EPS) * g_ref[...]


def _final_norm(h, g):
    n_tok = h.shape[0]
    tm = min(512, n_tok)
    return pl.pallas_call(
        _final_norm_kernel,
        grid=(n_tok // tm,),
        in_specs=[_row_spec(tm, D), _full_spec(g)],
        out_specs=_row_spec(tm, D),
        out_shape=jax.ShapeDtypeStruct((n_tok, D), F32),
        compiler_params=_params(),
        name="final_norm",
    )(h, g)


def _block_diag(w):
    n_h, a, b = w.shape[-3:]
    out = w[..., :, :, None, :] * jnp.eye(n_h, dtype=w.dtype)[:, None, :, None]
    return out.reshape(w.shape[:-3] + (n_h * a, n_h * b))


def _pad_lora(w):
    z = jnp.zeros_like(w[:, 0])
    return jnp.stack([jnp.concatenate([w[:, 0], z], axis=1), jnp.concatenate([z, w[:, 1]], axis=1)], axis=1)


def _to_s5_blocks(u, col_major):
    n_tok = u.shape[0]
    if col_major:
        rows = n_tok // GRID_W
        x = u.reshape(rows // S5_CHUNK, S5_CHUNK, GRID_W, S5_GROUPS, S5_GROUP)
        x = x.transpose(3, 2, 0, 1, 4)
    else:
        x = u.reshape(n_tok // S5_CHUNK, S5_CHUNK, S5_GROUPS, S5_GROUP).transpose(2, 0, 1, 3)
    return x.reshape(S5_GROUPS, n_tok // S5_CHUNK, S5_CHUNK * S5_GROUP)


def _from_s5_blocks(y, n_tok, col_major):
    if col_major:
        rows = n_tok // GRID_W
        x = y.reshape(S5_GROUPS, GRID_W, rows // S5_CHUNK, S5_CHUNK, S5_GROUP).transpose(2, 3, 1, 0, 4)
    else:
        x = y.reshape(S5_GROUPS, n_tok // S5_CHUNK, S5_CHUNK, S5_GROUP).transpose(1, 2, 0, 3)
    return x.reshape(n_tok, WIDTH)


def _prepare(c, c_ctx, w_mod, b_mod, norm1, norm2, w_in, lru_conv_w, lru_conv_b, lru_wa, lru_ba, lru_wx, lru_bx,
             lru_lam, rwkv_mu, rwkv_w0, rwkv_w2, rwkv_a0, rwkv_a2, rwkv_g2, rwkv_kk, rwkv_ka, rwkv_rk, rwkv_lnw,
             rwkv_lnb, s5_lam_re, s5_lam_im, s5_log_step, s5_b_re, s5_b_im, s5_c_re, s5_c_im, s5_d, s5_w_glu,
             s5_b_glu, w_branch, w_out, w_ffn_in, w_ffn_out):
    depth = w_in.shape[0]
    vec = lambda a: a.reshape(depth, 1, a.shape[-1])
    cond = jnp.concatenate([c, c_ctx[None], jnp.zeros((SUBLANE_TILE - 2, D), F32)], axis=0)
    w_in_bf = _bf(w_in)
    s5_w, s5_v, s5_lam16 = _s5_weights(s5_lam_re, s5_lam_im, s5_log_step, s5_b_re, s5_b_im, s5_c_re, s5_c_im)
    head_id = jnp.arange(WIDTH) // HEAD
    return dict(
        mod=_modulation(cond, w_mod, b_mod),
        w_a=w_in_bf[:, :, :N_A], w_zg=w_in_bf[:, :, N_A:],
        w_gate=_bf(jnp.concatenate([_block_diag(lru_wa), _block_diag(lru_wx)], axis=-1)),
        b_gate=jnp.concatenate([lru_ba, lru_bx], axis=-1)[:, :, None, :],
        lam=lru_lam[:, :, None, :],
        w2p=_bf(_pad_lora(rwkv_w2)), a2p=_bf(_pad_lora(rwkv_a2)),
        w0=rwkv_w0[:, :, None, :], a0=rwkv_a0[:, :, None, :], rk=rwkv_rk.reshape(depth, 1, WIDTH),
        s5_w=s5_w, s5_v=s5_v, s5_lam16=s5_lam16,
        w_branch=_bf(w_branch), w_out=_bf(w_out), w_ffn_in=_bf(w_ffn_in), w_ffn_out=_bf(w_ffn_out),
        g2=_bf(rwkv_g2), w_glu=_bf(s5_w_glu), n1=vec(norm1), n2=vec(norm2), conv_w=lru_conv_w,
        conv_b=vec(lru_conv_b), mu=vec(rwkv_mu), kk_w=vec(rwkv_kk), ka=vec(rwkv_ka), lnw=vec(rwkv_lnw),
        lnb=vec(rwkv_lnb), dsk=vec(s5_d), b_glu=vec(s5_b_glu),
        bd_ones=(head_id[:, None] == head_id[None, :]).astype(BF16),
    )


def _zero_states():
    return (jnp.zeros((2, 1, WIDTH), F32),
            jnp.zeros((2, WIDTH // QUAD, QUAD, QUAD), F32),
            jnp.zeros((2, S5_GROUPS, 1, LANE_TILE), F32))


def _mixers(p, h, layer, row, states, col_major):
    n_tok = h.shape[0]
    xc, ga, zbs, uc, kkn = _premix(h, layer, row, p["mod"], p["n1"], p["w_a"], p["conv_w"], p["conv_b"],
                                   p["mu"], p["kk_w"], p["bd_ones"])
    hf, hr, fin_lru = _lru(xc, layer, p["w_gate"], p["b_gate"], p["lam"], states[0])
    yf, yr, bf_, br_, fin_wkv = _wkv(zbs, kkn, layer, p["w2p"], p["a2p"], p["w0"], p["a0"], p["ka"], p["rk"],
                                     p["bd_ones"], states[1])
    ys5, fin_s5 = _s5(_to_s5_blocks(uc, col_major), layer, p["s5_w"], p["s5_v"], p["s5_lam16"], states[2])
    ys5 = _from_s5_blocks(ys5, n_tok, col_major)
    return (hf, hr, ga, yf, yr, bf_, br_, zbs, ys5, uc), (fin_lru, fin_wkv, fin_s5)


def _channel_mix(p, h, parts, layer, row):
    h = _merge(h, parts, layer, row, p["mod"], p["n1"], p["w_zg"], p["lnw"], p["lnb"], p["g2"], p["dsk"],
               p["w_glu"], p["b_glu"], p["w_branch"], p["w_out"], p["bd_ones"])
    return _ffn(h, layer, row, p["mod"], p["n2"], p["w_ffn_in"], p["w_ffn_out"])


def kernel(x, c, ctx, c_ctx, w_mod, b_mod, norm1, norm2, norm_f, w_in, lru_conv_w, lru_conv_b, lru_wa, lru_ba,
           lru_wx, lru_bx, lru_lam, rwkv_mu, rwkv_w0, rwkv_w2, rwkv_a0, rwkv_a2, rwkv_g2, rwkv_kk, rwkv_ka,
           rwkv_rk, rwkv_lnw, rwkv_lnb, s5_lam_re, s5_lam_im, s5_log_step, s5_b_re, s5_b_im, s5_c_re, s5_c_im,
           s5_d, s5_w_glu, s5_b_glu, w_branch, w_out, w_ffn_in, w_ffn_out):
    bsz, n_lat, d_model = x.shape
    n_ctx = ctx.shape[1]
    depth = w_in.shape[0]
    assert bsz == 1 and d_model == D
    assert n_lat % (GRID_W * S5_CHUNK) == 0 and n_lat % 256 == 0
    assert n_ctx % WKV_CHUNK == 0 and (n_ctx <= 256 or n_ctx % 256 == 0)
    p = _prepare(c, c_ctx, w_mod, b_mod, norm1, norm2, w_in, lru_conv_w, lru_conv_b, lru_wa, lru_ba, lru_wx,
                 lru_bx, lru_lam, rwkv_mu, rwkv_w0, rwkv_w2, rwkv_a0, rwkv_a2, rwkv_g2, rwkv_kk, rwkv_ka, rwkv_rk,
                 rwkv_lnw, rwkv_lnb, s5_lam_re, s5_lam_im, s5_log_step, s5_b_re, s5_b_im, s5_c_re, s5_c_im, s5_d,
                 s5_w_glu, s5_b_glu, w_branch, w_out, w_ffn_in, w_ffn_out)
    h_lat, h_ctx = x[0], ctx[0]
    for layer in range(depth):
        parts_c, states = _mixers(p, h_ctx, layer, 1, _zero_states(), False)
        parts_l, _ = _mixers(p, h_lat, layer, 0, states, True)
        h_lat = _channel_mix(p, h_lat, parts_l, layer, 0)
        if layer != depth - 1:
            h_ctx = _channel_mix(p, h_ctx, parts_c, layer, 1)
    return _final_norm(h_lat, norm_f.reshape(1, D))[None]
```

```python
import functools
import math

import jax
import jax.numpy as jnp
from jax import lax
from jax.experimental import pallas as pl
from jax.experimental.pallas import tpu as pltpu

F32 = jnp.float32
BF16 = jnp.bfloat16
HIGHEST = lax.Precision.HIGHEST

D = 1024
WIDTH = 512
GRID_W = 64
EPS = 1e-6
LRU_C = 8.0
HEAD = 64
RWKV_IN = 3 * WIDTH + 2 * 64 + 2 * 64 + 128
GN_EPS = 64e-5
S5_GROUPS = 32
S5_GROUP = 16
S5_STATE = 64
S5_CHUNK = 16
N_A = 2 * WIDTH + RWKV_IN + WIDTH
FFN_HIDDEN = 2816
WKV_CHUNK = 64
LANE_TILE = 128
SUBLANE_TILE = 8
QUAD = 4 * HEAD
VMEM_LIMIT = 56 * 1024 * 1024


def _bf(x):
    return x.astype(BF16)


def _mm(a, b):
    return jnp.dot(_bf(a), _bf(b), preferred_element_type=F32)


def _mm_nt(a, b):
    return lax.dot_general(_bf(a), _bf(b), (((1,), (1,)), ((), ())), preferred_element_type=F32)


def _mm_tn(a, b):
    return lax.dot_general(_bf(a), _bf(b), (((0,), (0,)), ((), ())), preferred_element_type=F32)


def _softplus(x):
    return jnp.maximum(x, 0.0) + jnp.log(1.0 + jnp.exp(-jnp.abs(x)))


def _head_sum(x, bd_ones):
    hi = _bf(x)
    lo = _bf(x - hi.astype(F32))
    return (jnp.dot(hi, bd_ones, preferred_element_type=F32)
            + jnp.dot(lo, bd_ones, preferred_element_type=F32))


def _mod_norm(x, g, shift, scale):
    y = x * lax.rsqrt(jnp.mean(x * x, axis=-1, keepdims=True) + EPS) * g
    return y * (1.0 + scale) + shift


def _params(n_axes=1):
    return pltpu.CompilerParams(dimension_semantics=("arbitrary",) * n_axes,
                                vmem_limit_bytes=VMEM_LIMIT)


def _layer_spec(arr, layer):
    rest = arr.shape[1:]
    return pl.BlockSpec((None,) + rest, lambda *_: (layer,) + (0,) * len(rest))


def _full_spec(arr):
    return pl.BlockSpec(arr.shape, lambda *_: (0,) * arr.ndim)


def _row_spec(tm, width, col=0):
    return pl.BlockSpec((tm, width), lambda i: (i, col))


def _mod_kernel(cond_ref, w_ref, b_ref, o_ref):
    cnd = cond_ref[...]
    act = cnd * jax.nn.sigmoid(cnd)
    o_ref[...] = jnp.dot(act, w_ref[...], preferred_element_type=F32, precision=HIGHEST) + b_ref[...]


def _modulation(cond, w_mod, b_mod):
    depth = w_mod.shape[0]
    n_col = w_mod.shape[2] // D
    return pl.pallas_call(
        _mod_kernel,
        grid=(depth, n_col),
        in_specs=[pl.BlockSpec((SUBLANE_TILE, D), lambda l, j: (0, 0)),
                  pl.BlockSpec((None, D, D), lambda l, j: (l, 0, j)),
                  pl.BlockSpec((None, 1, D), lambda l, j: (l, 0, j))],
        out_specs=pl.BlockSpec((None, SUBLANE_TILE, D), lambda l, j: (l, 0, j)),
        out_shape=jax.ShapeDtypeStruct((depth, SUBLANE_TILE, 6 * D), F32),
        compiler_params=_params(2),
        name="modulation",
    )(cond, w_mod, b_mod.reshape(depth, 1, 6 * D))


def _premix_kernel(h_ref, hp_ref, hn_ref, mod_ref, g_ref, w_ref, cw_ref, cb_ref, mu_ref, kk_ref, bd_ref,
                   xc_ref, ga_ref, zb_ref, uc_ref, kkn_ref, z_scr, *, row, tm):
    i = pl.program_id(0)
    n = pl.num_programs(0)
    halo = SUBLANE_TILE
    shift = mod_ref[row:row + 1, 0:D]
    scale = mod_ref[row:row + 1, D:2 * D]
    hext = jnp.concatenate([hp_ref[...], h_ref[...], hn_ref[...]], axis=0)
    xn = _mod_norm(hext, g_ref[...], shift, scale)
    z_scr[...] = _mm(xn, w_ref[...])

    @pl.when(i == 0)
    def _():
        z_scr[0:halo, :] = jnp.zeros((halo, N_A), F32)

    @pl.when(i == n - 1)
    def _():
        z_scr[tm + halo:tm + 2 * halo, :] = jnp.zeros((halo, N_A), F32)

    acc = cb_ref[...] + cw_ref[0:1, :] * z_scr[halo - 2:halo - 2 + tm, 0:WIDTH]
    for j in range(1, 4):
        acc = acc + cw_ref[j:j + 1, :] * z_scr[halo - 2 + j:halo - 2 + j + tm, 0:WIDTH]
    xc_ref[...] = acc
    ga_ref[...] = z_scr[halo:halo + tm, WIDTH:2 * WIDTH]
    lo, hi = 2 * WIDTH, 2 * WIDTH + RWKV_IN
    zc = z_scr[halo:halo + tm, lo:hi]
    zp = z_scr[halo - 1:halo - 1 + tm, lo:hi]
    zn = z_scr[halo + 1:halo + 1 + tm, lo:hi]
    zs = zc + mu_ref[...] * (0.5 * (zp + zn) - zc)
    zb_ref[...] = zs
    uc_ref[...] = z_scr[halo:halo + tm, hi:hi + WIDTH]
    kk = zs[:, WIDTH:2 * WIDTH] * kk_ref[...]
    ss = _head_sum(kk * kk, bd_ref[...])
    kkn_ref[...] = kk * lax.rsqrt(ss + 1e-12)


def _premix(h, layer, row, mod, norm1, w_a, conv_w, conv_b, mu, kk_w, bd_ones):
    n_tok = h.shape[0]
    tm = min(256, n_tok)
    n = n_tok // tm
    per = tm // SUBLANE_TILE
    last_blk = n_tok // SUBLANE_TILE - 1
    kern = functools.partial(_premix_kernel, row=row, tm=tm)
    outs = pl.pallas_call(
        kern,
        grid=(n,),
        in_specs=[
            _row_spec(tm, D),
            pl.BlockSpec((SUBLANE_TILE, D), lambda i: (jnp.maximum(i * per - 1, 0), 0)),
            pl.BlockSpec((SUBLANE_TILE, D), lambda i: (jnp.minimum((i + 1) * per, last_blk), 0)),
            _layer_spec(mod, layer), _layer_spec(norm1, layer), _layer_spec(w_a, layer),
            _layer_spec(conv_w, layer), _layer_spec(conv_b, layer), _layer_spec(mu, layer),
            _layer_spec(kk_w, layer), _full_spec(bd_ones),
        ],
        out_specs=[_row_spec(tm, WIDTH), _row_spec(tm, WIDTH), _row_spec(tm, RWKV_IN),
                   _row_spec(tm, WIDTH), _row_spec(tm, WIDTH)],
        out_shape=[jax.ShapeDtypeStruct((n_tok, WIDTH), F32), jax.ShapeDtypeStruct((n_tok, WIDTH), F32),
                   jax.ShapeDtypeStruct((n_tok, RWKV_IN), F32), jax.ShapeDtypeStruct((n_tok, WIDTH), F32),
                   jax.ShapeDtypeStruct((n_tok, WIDTH), F32)],
        scratch_shapes=[pltpu.VMEM((tm + 2 * SUBLANE_TILE, N_A), F32)],
        compiler_params=_params(),
        name="premix",
    )(h, h, h, mod, norm1, w_a, conv_w, conv_b, mu, kk_w, bd_ones)
    return outs


def _lru_kernel(xf_ref, xr_ref, wg_ref, bg_ref, lam_ref, h0_ref, hf_ref, hr_ref, fin_ref, carry, *, tm):
    i = pl.program_id(0)

    @pl.when(i == 0)
    def _():
        carry[...] = h0_ref[...]

    rows = lax.broadcasted_iota(jnp.int32, (tm, WIDTH), 0)
    for d, (x_ref, o_ref) in enumerate(((xf_ref, hf_ref), (xr_ref, hr_ref))):
        xc = x_ref[...]
        gates = _mm(xc, wg_ref[d]) + bg_ref[d]
        gate_r = jax.nn.sigmoid(gates[:, 0:WIDTH])
        gate_i = jax.nn.sigmoid(gates[:, WIDTH:2 * WIDTH])
        log_a = -LRU_C * gate_r * _softplus(-lam_ref[d])
        a = jnp.exp(log_a)
        b = jnp.sqrt(1.0 - jnp.exp(2.0 * log_a)) * (gate_i * xc)
        s = 1
        while s < tm:
            if d == 0:
                a_sh, b_sh, m = pltpu.roll(a, s, 0), pltpu.roll(b, s, 0), rows >= s
            else:
                a_sh, b_sh, m = pltpu.roll(a, tm - s, 0), pltpu.roll(b, tm - s, 0), rows < tm - s
            b = jnp.where(m, a * b_sh + b, b)
            a = jnp.where(m, a * a_sh, a)
            s *= 2
        hs = a * carry[d] + b
        o_ref[...] = hs
        carry[d] = hs[tm - 1:tm, :] if d == 0 else hs[0:1, :]
    fin_ref[...] = carry[...]


def _lru(xc, layer, w_gate, b_gate, lam, h0):
    n_tok = xc.shape[0]
    tm = min(256, n_tok)
    n = n_tok // tm
    kern = functools.partial(_lru_kernel, tm=tm)
    return pl.pallas_call(
        kern,
        grid=(n,),
        in_specs=[_row_spec(tm, WIDTH),
                  pl.BlockSpec((tm, WIDTH), lambda i: (n - 1 - i, 0)),
                  _layer_spec(w_gate, layer), _layer_spec(b_gate, layer), _layer_spec(lam, layer),
                  _full_spec(h0)],
        out_specs=[_row_spec(tm, WIDTH),
                   pl.BlockSpec((tm, WIDTH), lambda i: (n - 1 - i, 0)),
                   pl.BlockSpec((2, 1, WIDTH), lambda i: (0, 0, 0))],
        out_shape=[jax.ShapeDtypeStruct((n_tok, WIDTH), F32), jax.ShapeDtypeStruct((n_tok, WIDTH), F32),
                   jax.ShapeDtypeStruct((2, 1, WIDTH), F32)],
        scratch_shapes=[pltpu.VMEM((2, 1, WIDTH), F32)],
        compiler_params=_params(),
        name="rglru_scan",
    )(xc, xc, w_gate, b_gate, lam, h0)


def _block_rows(x, lane_head):
    return jnp.concatenate([jnp.where(lane_head == h, x, 0.0) for h in range(QUAD // HEAD)], axis=0)


def _interleave(gens):
    results = [None] * len(gens)
    active = list(range(len(gens)))
    while active:
        still = []
        for g in active:
            try:
                next(gens[g])
                still.append(g)
            except StopIteration as stop:
                results[g] = stop.value
        active = still
    return results


def _wkv_kernel(zf_ref, zr_ref, kf_ref, kr_ref, w2_ref, a2_ref, w0_ref, a0_ref, ka_ref, rk_ref, bd_ref, s0_ref,
                yf_ref, yr_ref, bf_ref, br_ref, sfin_ref, s_scr, *, n_chunk):
    i = pl.program_id(0)
    t = WKV_CHUNK

    @pl.when(i == 0)
    def _():
        s_scr[...] = s0_ref[...]

    row = lax.broadcasted_iota(jnp.int32, (t, QUAD), 0)
    lane = lax.broadcasted_iota(jnp.int32, (t, QUAD), 1)
    lane_head = lane >> 6
    lane_tok = lane & (HEAD - 1)
    row_t = lax.broadcasted_iota(jnp.int32, (t, t), 0)
    col_t = lax.broadcasted_iota(jnp.int32, (t, t), 1)
    same_head = (lax.broadcasted_iota(jnp.int32, (QUAD, QUAD), 0) >> 6) == (
        lax.broadcasted_iota(jnp.int32, (QUAD, QUAD), 1) >> 6)
    eye = jnp.where(lane_tok == row, 1.0, 0.0)
    pair_mask = [((row ^ lane_tok) >> lvl) == 1 for lvl in range(int(math.log2(t)))]

    n_quad = WIDTH // QUAD
    bd = functools.partial(_block_rows, lane_head=lane_head)

    def chunk_setup(d, at, rt, bt, kt, vq):
        strict, incl = (lane_tok < row, lane_tok <= row) if d == 0 else (lane_tok > row, lane_tok >= row)
        ar = jnp.concatenate([at, rt], axis=0)
        gb = _mm_nt(ar, bd(bt))
        yield
        gk = _mm_nt(ar, bd(kt))
        yield
        a_ab = jnp.where(strict, gb[0:t], 0.0)
        a_rb = jnp.where(incl, gb[t:2 * t], 0.0)
        a_ak = jnp.where(strict, gk[0:t], 0.0)
        a_rk = jnp.where(incl, gk[t:2 * t], 0.0)
        inv = eye + jnp.where(pair_mask[0], a_ab, 0.0)
        for lvl in range(1, len(pair_mask)):
            half = _mm(inv, bd(jnp.where(pair_mask[lvl], a_ab, 0.0)))
            yield
            inv = inv + _mm(half, bd(inv))
            yield
        v_bd = bd(vq)
        x0 = _mm(a_ak, v_bd)
        yield
        y0 = _mm(a_rk, v_bd)
        yield
        wu = _mm(inv, jnp.concatenate([bd(at), bd(x0)], axis=1))
        yield
        return wu[:, 0:QUAD], wu[:, QUAD:2 * QUAD], a_rb, y0

    def state_chain(d, q, order, pre, post, y_ref):
        s = s_scr[d, q]
        for c in order:
            w1, u0, a_rb, y0 = pre[(d, c, q)]
            rt, vq, b_end, k_end, decay = post[(d, c, q)]
            u = _mm_nt(w1, s) + u0
            yield
            y = _mm_nt(rt, s) + y0
            yield
            y = y + _mm(a_rb, bd(u))
            yield
            upd = _mm_tn(jnp.concatenate([u, vq], axis=0), jnp.concatenate([b_end, k_end], axis=0))
            yield
            s = s * decay + jnp.where(same_head, upd, 0.0)
            y_ref[c * t:(c + 1) * t, q * QUAD:(q + 1) * QUAD] = y
        s_scr[d, q] = s

    setups, post = {}, {}
    dirs = ((zf_ref, kf_ref, yf_ref, bf_ref), (zr_ref, kr_ref, yr_ref, br_ref))
    for d, (z_ref, k_ref, y_ref, b_ref) in enumerate(dirs):
        r = z_ref[:, 0:WIDTH]
        k = z_ref[:, WIDTH:2 * WIDTH]
        v = z_ref[:, 2 * WIDTH:3 * WIDTH]
        wd = z_ref[:, 3 * WIDTH:3 * WIDTH + LANE_TILE]
        ad = z_ref[:, 3 * WIDTH + LANE_TILE:3 * WIDTH + 2 * LANE_TILE]
        kkn = k_ref[...]
        w_log = -_softplus(-(w0_ref[d] + _mm(jnp.tanh(wd), w2_ref[d]))) - 0.5
        lw = -jnp.exp(w_log)
        iclr = jax.nn.sigmoid(a0_ref[d] + _mm(ad, a2_ref[d]))
        kd = k * (1.0 + (iclr - 1.0) * ka_ref[...])
        kb = kkn * iclr
        b_ref[...] = _head_sum(r * kd * rk_ref[...], bd_ref[...]) * v
        p1 = _bf(lw)
        r1 = lw - p1.astype(F32)
        p2 = _bf(r1)
        p3 = _bf(r1 - p2.astype(F32))
        lw3 = jnp.concatenate([p1, p2, p3], axis=1)
        tri = jnp.where(col_t <= row_t if d == 0 else col_t >= row_t, 1.0, 0.0).astype(BF16)
        for c in range(n_chunk):
            rows_c = slice(c * t, (c + 1) * t)
            cum3 = jnp.dot(tri, lw3[rows_c], preferred_element_type=F32)
            cum = cum3[:, 0:WIDTH] + cum3[:, WIDTH:2 * WIDTH] + cum3[:, 2 * WIDTH:3 * WIDTH]
            tot = cum[t - 1:t, :] if d == 0 else cum[0:1, :]
            e_neg = jnp.exp(-cum)
            e_end = jnp.exp(tot - cum)
            a_t = -kkn[rows_c] * jnp.exp(cum - lw[rows_c])
            r_t = r[rows_c] * jnp.exp(cum)
            b_t = kb[rows_c] * e_neg
            k_t = kd[rows_c] * e_neg
            b_end = kb[rows_c] * e_end
            k_end = kd[rows_c] * e_end
            decay_tot = jnp.exp(tot)
            for q in range(n_quad):
                sl = slice(q * QUAD, (q + 1) * QUAD)
                vq = v[rows_c, sl]
                setups[(d, c, q)] = chunk_setup(d, a_t[:, sl], r_t[:, sl], b_t[:, sl], k_t[:, sl], vq)
                post[(d, c, q)] = (r_t[:, sl], vq, b_end[:, sl], k_end[:, sl], decay_tot[:, sl])

    keys = list(setups)
    pre = dict(zip(keys, _interleave([setups[key] for key in keys])))
    chains = []
    for d, y_ref in ((0, yf_ref), (1, yr_ref)):
        order = range(n_chunk) if d == 0 else range(n_chunk - 1, -1, -1)
        chains += [state_chain(d, q, order, pre, post, y_ref) for q in range(n_quad)]
    _interleave(chains)
    sfin_ref[...] = s_scr[...]


def _wkv(zbs, kkn, layer, w2p, a2p, w0, a0, ka, rk, bd_ones, s0):
    n_tok = zbs.shape[0]
    t = min(256, n_tok)
    n = n_tok // t
    fwd = lambda i: (i, 0)
    rev = lambda i: (n - 1 - i, 0)
    state_shape = s0.shape
    return pl.pallas_call(
        functools.partial(_wkv_kernel, n_chunk=t // WKV_CHUNK),
        grid=(n,),
        in_specs=[pl.BlockSpec((t, RWKV_IN), fwd), pl.BlockSpec((t, RWKV_IN), rev),
                  pl.BlockSpec((t, WIDTH), fwd), pl.BlockSpec((t, WIDTH), rev),
                  _layer_spec(w2p, layer), _layer_spec(a2p, layer), _layer_spec(w0, layer),
                  _layer_spec(a0, layer), _layer_spec(ka, layer), _layer_spec(rk, layer),
                  _full_spec(bd_ones), _full_spec(s0)],
        out_specs=[pl.BlockSpec((t, WIDTH), fwd), pl.BlockSpec((t, WIDTH), rev),
                   pl.BlockSpec((t, WIDTH), fwd), pl.BlockSpec((t, WIDTH), rev),
                   pl.BlockSpec(state_shape, lambda i: (0, 0, 0, 0))],
        out_shape=[jax.ShapeDtypeStruct((n_tok, WIDTH), F32)] * 4 + [jax.ShapeDtypeStruct(state_shape, F32)],
        scratch_shapes=[pltpu.VMEM(state_shape, F32)],
        compiler_params=_params(),
        name="wkv7_scan",
    )(zbs, zbs, kkn, kkn, w2p, a2p, w0, a0, ka, rk, bd_ones, s0)


def _cmul_rows(x, p):
    half = LANE_TILE // 2
    lane = lax.broadcasted_iota(jnp.int32, p.shape, 1)
    p_sw = pltpu.roll(p, half, 1)
    p1 = jnp.where(lane < half, p, p_sw)[0:1, :]
    p2 = jnp.where(lane < half, -p_sw, p)[0:1, :]
    return x * p1 + pltpu.roll(x, half, 1) * p2


def _s5_kernel(u_ref, w_ref, v_ref, lam_ref, h0_ref, y_ref, fin_ref, *, nc):
    blk = S5_CHUNK * S5_GROUP
    res = _mm(u_ref[...], w_ref[...])
    row = lax.broadcasted_iota(jnp.int32, (nc, LANE_TILE), 0)
    h_in = []
    for d in range(2):
        e = res[:, blk + d * LANE_TILE:blk + (d + 1) * LANE_TILE]
        lam = jnp.broadcast_to(lam_ref[d], (SUBLANE_TILE, LANE_TILE))
        h0 = jnp.broadcast_to(h0_ref[d], (SUBLANE_TILE, LANE_TILE))
        first = 0 if d == 0 else nc - 1
        e = jnp.where(row == first, e + _cmul_rows(h0, lam)[0:1, :], e)
        p = lam
        s = 1
        while s < nc:
            if d == 0:
                sh, m = pltpu.roll(e, s, 0), row >= s
            else:
                sh, m = pltpu.roll(e, nc - s, 0), row < nc - s
            e = e + jnp.where(m, _cmul_rows(sh, p), 0.0)
            p = _cmul_rows(p, p)
            s *= 2
        if d == 0:
            fin_ref[d] = e[nc - 1:nc, :]
            h_in.append(jnp.where(row >= 1, pltpu.roll(e, 1, 0), h0[0:1, :]))
        else:
            fin_ref[d] = e[0:1, :]
            h_in.append(jnp.where(row < nc - 1, pltpu.roll(e, nc - 1, 0), h0[0:1, :]))
    y_ref[...] = res[:, 0:blk] + _mm(jnp.concatenate(h_in, axis=1), v_ref[...])


def _s5(u_blocks, layer, w_cat, v_cat, lam16, h0):
    groups, nc, blk = u_blocks.shape
    kern = functools.partial(_s5_kernel, nc=nc)
    return pl.pallas_call(
        kern,
        grid=(groups,),
        in_specs=[pl.BlockSpec((None, nc, blk), lambda g: (g, 0, 0)),
                  pl.BlockSpec((None, None, blk, w_cat.shape[-1]), lambda g: (layer, g, 0, 0)),
                  pl.BlockSpec((None, None, blk, blk), lambda g: (layer, g, 0, 0)),
                  pl.BlockSpec((None, 2, None, 1, LANE_TILE), lambda g: (layer, 0, g, 0, 0)),
                  pl.BlockSpec((2, None, 1, LANE_TILE), lambda g: (0, g, 0, 0))],
        out_specs=[pl.BlockSpec((None, nc, blk), lambda g: (g, 0, 0)),
                   pl.BlockSpec((2, None, 1, LANE_TILE), lambda g: (0, g, 0, 0))],
        out_shape=[jax.ShapeDtypeStruct((groups, nc, blk), F32),
                   jax.ShapeDtypeStruct((2, groups, 1, LANE_TILE), F32)],
        compiler_params=_params(),
        name="s5_scan",
    )(u_blocks, w_cat, v_cat, lam16, h0)


def _s5_weights(lam_re, lam_im, log_step, b_re, b_im, c_re, c_im):
    n = S5_CHUNK
    step = jnp.exp(log_step)[..., None]
    x_re, ang = lam_re * step, lam_im * step
    mag = jnp.exp(x_re)
    lb_re, lb_im = mag * jnp.cos(ang), mag * jnp.sin(ang)
    nr = jnp.expm1(x_re) * jnp.cos(ang) - 2.0 * jnp.square(jnp.sin(0.5 * ang))
    den = lam_re * lam_re + lam_im * lam_im
    f_re = (nr * lam_re + lb_im * lam_im) / den
    f_im = (lb_im * lam_re - nr * lam_im) / den
    bb_re = f_re[..., None] * b_re - f_im[..., None] * b_im
    bb_im = f_re[..., None] * b_im + f_im[..., None] * b_re
    pr, pi = [jnp.ones_like(lb_re)], [jnp.zeros_like(lb_re)]
    for _ in range(n):
        pr, pi = pr + [pr[-1] * lb_re - pi[-1] * lb_im], pi + [pr[-1] * lb_im + pi[-1] * lb_re]
    pw_re, pw_im = jnp.stack(pr, axis=-2), jnp.stack(pi, axis=-2)
    cp_re = c_re[..., None, :, :] * pw_re[..., :, None, :] - c_im[..., None, :, :] * pw_im[..., :, None, :]
    cp_im = c_re[..., None, :, :] * pw_im[..., :, None, :] + c_im[..., None, :, :] * pw_re[..., :, None, :]
    taps = (jnp.einsum('...jcp,...pk->...jck', cp_re, bb_re, precision=HIGHEST)
            - jnp.einsum('...jcp,...pk->...jck', cp_im, bb_im, precision=HIGHEST))
    s_idx = jnp.arange(n)[:, None]
    t_idx = jnp.arange(n)[None, :]

    def toeplitz(k, lag):
        m = jnp.take(k, jnp.clip(lag, 0, n).reshape(-1), axis=-3)
        m = m.reshape(k.shape[:-3] + (n, n, S5_GROUP, S5_GROUP))
        m = jnp.where((lag >= 0)[:, :, None, None], m, 0.0)
        m = jnp.swapaxes(jnp.swapaxes(m, -1, -2), -2, -3)
        return m.reshape(k.shape[:-3] + (n * S5_GROUP, n * S5_GROUP))

    m_tot = toeplitz(taps[:, 0], t_idx - s_idx) + toeplitz(taps[:, 1], s_idx - t_idx)

    def end_state(d, order):
        qr, qi = pw_re[:, d][..., order, :], pw_im[:, d][..., order, :]
        er = qr[..., :, None, :] * jnp.swapaxes(bb_re[:, d], -1, -2)[..., None, :, :] \
            - qi[..., :, None, :] * jnp.swapaxes(bb_im[:, d], -1, -2)[..., None, :, :]
        ei = qr[..., :, None, :] * jnp.swapaxes(bb_im[:, d], -1, -2)[..., None, :, :] \
            + qi[..., :, None, :] * jnp.swapaxes(bb_re[:, d], -1, -2)[..., None, :, :]
        e = jnp.concatenate([er, ei], axis=-1)
        return e.reshape(e.shape[:-3] + (n * S5_GROUP, 2 * S5_STATE))

    def state_out(d, order):
        vr = jnp.moveaxis(cp_re[:, d][..., order, :, :], -1, -3)
        vi = -jnp.moveaxis(cp_im[:, d][..., order, :, :], -1, -3)
        vv = jnp.concatenate([vr, vi], axis=-3)
        return vv.reshape(vv.shape[:-3] + (2 * S5_STATE, n * S5_GROUP))

    asc = jnp.arange(n)
    w_cat = jnp.concatenate([m_tot, end_state(0, n - 1 - asc), end_state(1, asc)], axis=-1)
    v_cat = jnp.concatenate([state_out(0, asc + 1), state_out(1, n - asc)], axis=-2)
    lam16 = jnp.concatenate([pw_re[..., n, :], pw_im[..., n, :]], axis=-1)[..., None, :]
    return _bf(w_cat), _bf(v_cat), lam16


def _merge_kernel(h_ref, hf_ref, hr_ref, ga_ref, yf_ref, yr_ref, bf_ref, br_ref, gd_ref, ys_ref, uc_ref,
                  mod_ref, g_ref, wzg_ref, lnw_ref, lnb_ref, g2_ref, dsk_ref, wglu_ref, bglu_ref,
                  wbr_ref, wout_ref, bd_ref, o_ref, *, row):
    h = h_ref[...]
    shift = mod_ref[row:row + 1, 0:D]
    scale = mod_ref[row:row + 1, D:2 * D]
    gate = mod_ref[row:row + 1, 2 * D:3 * D]
    zg = _mm(_mod_norm(h, g_ref[...], shift, scale), wzg_ref[...])
    bd = bd_ref[...]
    y_a = jax.nn.gelu(ga_ref[...]) * (hf_ref[...] + hr_ref[...])
    y = yf_ref[...] + yr_ref[...]
    inv_n = 1.0 / HEAD
    yc = y - _head_sum(y, bd) * inv_n
    var = _head_sum(yc * yc, bd) * inv_n
    y_b = yc * lax.rsqrt(var + GN_EPS) * lnw_ref[...] + lnb_ref[...] + (bf_ref[...] + br_ref[...])
    y_b = y_b * _mm(jax.nn.sigmoid(gd_ref[...]), g2_ref[...])
    y_s = jax.nn.gelu(ys_ref[...] + dsk_ref[...] * uc_ref[...])
    y_c = y_s * jax.nn.sigmoid(_mm(y_s, wglu_ref[...]) + bglu_ref[...])
    mix = (jax.nn.sigmoid(zg[:, 0:D]) * _mm(y_a, wbr_ref[0])
           + jax.nn.sigmoid(zg[:, D:2 * D]) * _mm(y_b, wbr_ref[1])
           + jax.nn.sigmoid(zg[:, 2 * D:3 * D]) * _mm(y_c, wbr_ref[2]))
    o_ref[...] = h + gate * _mm(mix, wout_ref[...])


def _merge(h, parts, layer, row, mod, norm1, w_zg, lnw, lnb, g2, dsk, w_glu, b_glu, w_branch, w_out, bd_ones):
    n_tok = h.shape[0]
    tm = min(256, n_tok)
    hf, hr, ga, yf, yr, bf_, br_, zbs, ys5, uc = parts
    kern = functools.partial(_merge_kernel, row=row)
    rs = _row_spec(tm, WIDTH)
    gd_col = (3 * WIDTH + 2 * LANE_TILE) // LANE_TILE
    weights = (mod, norm1, w_zg, lnw, lnb, g2, dsk, w_glu, b_glu, w_branch, w_out)
    return pl.pallas_call(
        kern,
        grid=(n_tok // tm,),
        in_specs=[_row_spec(tm, D), rs, rs, rs, rs, rs, rs, rs, _row_spec(tm, LANE_TILE, gd_col), rs, rs]
        + [_layer_spec(w, layer) for w in weights] + [_full_spec(bd_ones)],
        out_specs=_row_spec(tm, D),
        out_shape=jax.ShapeDtypeStruct((n_tok, D), F32),
        compiler_params=_params(),
        name="merge_project",
    )(h, hf, hr, ga, yf, yr, bf_, br_, zbs, ys5, uc, *weights, bd_ones)


def _ffn_kernel(h_ref, mod_ref, g_ref, win_ref, wout_ref, o_ref, *, row):
    h = h_ref[...]
    shift = mod_ref[row:row + 1, 3 * D:4 * D]
    scale = mod_ref[row:row + 1, 4 * D:5 * D]
    gate = mod_ref[row:row + 1, 5 * D:6 * D]
    gu = _mm(_mod_norm(h, g_ref[...], shift, scale), win_ref[...])
    a, up = gu[:, 0:FFN_HIDDEN], gu[:, FFN_HIDDEN:2 * FFN_HIDDEN]
    o_ref[...] = h + gate * _mm(a * jax.nn.sigmoid(a) * up, wout_ref[...])


def _ffn(h, layer, row, mod, norm2, w_in, w_out):
    n_tok = h.shape[0]
    tm = min(256, n_tok)
    kern = functools.partial(_ffn_kernel, row=row)
    return pl.pallas_call(
        kern,
        grid=(n_tok // tm,),
        in_specs=[_row_spec(tm, D), _layer_spec(mod, layer), _layer_spec(norm2, layer),
                  _layer_spec(w_in, layer), _layer_spec(w_out, layer)],
        out_specs=_row_spec(tm, D),
        out_shape=jax.ShapeDtypeStruct((n_tok, D), F32),
        compiler_params=_params(),
        name="swiglu",
    )(h, mod, norm2, w_in, w_out)


def _final_norm_kernel(h_ref, g_ref, o_ref):
    x = h_ref[...]
    o_ref[...] = x * lax.rsqrt(jnp.mean(x * x, axis=-1, keepdims=True) + EPS) * g_ref[...]


def _final_norm(h, g):
    n_tok = h.shape[0]
    tm = min(512, n_tok)
    return pl.pallas_call(
        _final_norm_kernel,
        grid=(n_tok // tm,),
        in_specs=[_row_spec(tm, D), _full_spec(g)],
        out_specs=_row_spec(tm, D),
        out_shape=jax.ShapeDtypeStruct((n_tok, D), F32),
        compiler_params=_params(),
        name="final_norm",
    )(h, g)


def _block_diag(w):
    n_h, a, b = w.shape[-3:]
    out = w[..., :, :, None, :] * jnp.eye(n_h, dtype=w.dtype)[:, None, :, None]
    return out.reshape(w.shape[:-3] + (n_h * a, n_h * b))


def _pad_lora(w):
    z = jnp.zeros_like(w[:, 0])
    return jnp.stack([jnp.concatenate([w[:, 0], z], axis=1), jnp.concatenate([z, w[:, 1]], axis=1)], axis=1)


def _to_s5_blocks(u, col_major):
    n_tok = u.shape[0]
    if col_major:
        rows = n_tok // GRID_W
        x = u.reshape(rows // S5_CHUNK, S5_CHUNK, GRID_W, S5_GROUPS, S5_GROUP)
        x = x.transpose(3, 2, 0, 1, 4)
    else:
        x = u.reshape(n_tok // S5_CHUNK, S5_CHUNK, S5_GROUPS, S5_GROUP).transpose(2, 0, 1, 3)
    return x.reshape(S5_GROUPS, n_tok // S5_CHUNK, S5_CHUNK * S5_GROUP)


def _from_s5_blocks(y, n_tok, col_major):
    if col_major:
        rows = n_tok // GRID_W
        x = y.reshape(S5_GROUPS, GRID_W, rows // S5_CHUNK, S5_CHUNK, S5_GROUP).transpose(2, 3, 1, 0, 4)
    else:
        x = y.reshape(S5_GROUPS, n_tok // S5_CHUNK, S5_CHUNK, S5_GROUP).transpose(1, 2, 0, 3)
    return x.reshape(n_tok, WIDTH)


def _prepare(c, c_ctx, w_mod, b_mod, norm1, norm2, w_in, lru_conv_w, lru_conv_b, lru_wa, lru_ba, lru_wx, lru_bx,
             lru_lam, rwkv_mu, rwkv_w0, rwkv_w2, rwkv_a0, rwkv_a2, rwkv_g2, rwkv_kk, rwkv_ka, rwkv_rk, rwkv_lnw,
             rwkv_lnb, s5_lam_re, s5_lam_im, s5_log_step, s5_b_re, s5_b_im, s5_c_re, s5_c_im, s5_d, s5_w_glu,
             s5_b_glu, w_branch, w_out, w_ffn_in, w_ffn_out):
    depth = w_in.shape[0]
    vec = lambda a: a.reshape(depth, 1, a.shape[-1])
    cond = jnp.concatenate([c, c_ctx[None], jnp.zeros((SUBLANE_TILE - 2, D), F32)], axis=0)
    w_in_bf = _bf(w_in)
    s5_w, s5_v, s5_lam16 = _s5_weights(s5_lam_re, s5_lam_im, s5_log_step, s5_b_re, s5_b_im, s5_c_re, s5_c_im)
    head_id = jnp.arange(WIDTH) // HEAD
    return dict(
        mod=_modulation(cond, w_mod, b_mod),
        w_a=w_in_bf[:, :, :N_A], w_zg=w_in_bf[:, :, N_A:],
        w_gate=_bf(jnp.concatenate([_block_diag(lru_wa), _block_diag(lru_wx)], axis=-1)),
        b_gate=jnp.concatenate([lru_ba, lru_bx], axis=-1)[:, :, None, :],
        lam=lru_lam[:, :, None, :],
        w2p=_bf(_pad_lora(rwkv_w2)), a2p=_bf(_pad_lora(rwkv_a2)),
        w0=rwkv_w0[:, :, None, :], a0=rwkv_a0[:, :, None, :], rk=rwkv_rk.reshape(depth, 1, WIDTH),
        s5_w=s5_w, s5_v=s5_v, s5_lam16=s5_lam16,
        w_branch=_bf(w_branch), w_out=_bf(w_out), w_ffn_in=_bf(w_ffn_in), w_ffn_out=_bf(w_ffn_out),
        g2=_bf(rwkv_g2), w_glu=_bf(s5_w_glu), n1=vec(norm1), n2=vec(norm2), conv_w=lru_conv_w,
        conv_b=vec(lru_conv_b), mu=vec(rwkv_mu), kk_w=vec(rwkv_kk), ka=vec(rwkv_ka), lnw=vec(rwkv_lnw),
        lnb=vec(rwkv_lnb), dsk=vec(s5_d), b_glu=vec(s5_b_glu),
        bd_ones=(head_id[:, None] == head_id[None, :]).astype(BF16),
    )


def _zero_states():
    return (jnp.zeros((2, 1, WIDTH), F32),
            jnp.zeros((2, WIDTH // QUAD, QUAD, QUAD), F32),
            jnp.zeros((2, S5_GROUPS, 1, LANE_TILE), F32))


def _mixers(p, h, layer, row, states, col_major):
    n_tok = h.shape[0]
    xc, ga, zbs, uc, kkn = _premix(h, layer, row, p["mod"], p["n1"], p["w_a"], p["conv_w"], p["conv_b"],
                                   p["mu"], p["kk_w"], p["bd_ones"])
    hf, hr, fin_lru = _lru(xc, layer, p["w_gate"], p["b_gate"], p["lam"], states[0])
    yf, yr, bf_, br_, fin_wkv = _wkv(zbs, kkn, layer, p["w2p"], p["a2p"], p["w0"], p["a0"], p["ka"], p["rk"],
                                     p["bd_ones"], states[1])
    ys5, fin_s5 = _s5(_to_s5_blocks(uc, col_major), layer, p["s5_w"], p["s5_v"], p["s5_lam16"], states[2])
    ys5 = _from_s5_blocks(ys5, n_tok, col_major)
    return (hf, hr, ga, yf, yr, bf_, br_, zbs, ys5, uc), (fin_lru, fin_wkv, fin_s5)


def _channel_mix(p, h, parts, layer, row):
    h = _merge(h, parts, layer, row, p["mod"], p["n1"], p["w_zg"], p["lnw"], p["lnb"], p["g2"], p["dsk"],
               p["w_glu"], p["b_glu"], p["w_branch"], p["w_out"], p["bd_ones"])
    return _ffn(h, layer, row, p["mod"], p["n2"], p["w_ffn_in"], p["w_ffn_out"])


def kernel(x, c, ctx, c_ctx, w_mod, b_mod, norm1, norm2, norm_f, w_in, lru_conv_w, lru_conv_b, lru_wa, lru_ba,
           lru_wx, lru_bx, lru_lam, rwkv_mu, rwkv_w0, rwkv_w2, rwkv_a0, rwkv_a2, rwkv_g2, rwkv_kk, rwkv_ka,
           rwkv_rk, rwkv_lnw, rwkv_lnb, s5_lam_re, s5_lam_im, s5_log_step, s5_b_re, s5_b_im, s5_c_re, s5_c_im,
           s5_d, s5_w_glu, s5_b_glu, w_branch, w_out, w_ffn_in, w_ffn_out):
    bsz, n_lat, d_model = x.shape
    n_ctx = ctx.shape[1]
    depth = w_in.shape[0]
    assert bsz == 1 and d_model == D
    assert n_lat % (GRID_W * S5_CHUNK) == 0 and n_lat % 256 == 0
    assert n_ctx % WKV_CHUNK == 0 and (n_ctx <= 256 or n_ctx % 256 == 0)
    p = _prepare(c, c_ctx, w_mod, b_mod, norm1, norm2, w_in, lru_conv_w, lru_conv_b, lru_wa, lru_ba, lru_wx,
                 lru_bx, lru_lam, rwkv_mu, rwkv_w0, rwkv_w2, rwkv_a0, rwkv_a2, rwkv_g2, rwkv_kk, rwkv_ka, rwkv_rk,
                 rwkv_lnw, rwkv_lnb, s5_lam_re, s5_lam_im, s5_log_step, s5_b_re, s5_b_im, s5_c_re, s5_c_im, s5_d,
                 s5_w_glu, s5_b_glu, w_branch, w_out, w_ffn_in, w_ffn_out)
    h_lat, h_ctx = x[0], ctx[0]
    for layer in range(depth):
        parts_c, states = _mixers(p, h_ctx, layer, 1, _zero_states(), False)
        parts_l, _ = _mixers(p, h_lat, layer, 0, states, True)
        h_lat = _channel_mix(p, h_lat, parts_l, layer, 0)
        if layer != depth - 1:
            h_ctx = _channel_mix(p, h_ctx, parts_c, layer, 1)
    return _final_norm(h_lat, norm_f.reshape(1, D))[None]
```

```python
import functools
import math

import jax
import jax.numpy as jnp
from jax import lax
from jax.experimental import pallas as pl
from jax.experimental.pallas import tpu as pltpu

F32 = jnp.float32
BF16 = jnp.bfloat16
HIGHEST = lax.Precision.HIGHEST

D = 1024
WIDTH = 512
GRID_W = 64
EPS = 1e-6
LRU_C = 8.0
HEAD = 64
RWKV_IN = 3 * WIDTH + 2 * 64 + 2 * 64 + 128
GN_EPS = 64e-5
S5_GROUPS = 32
S5_GROUP = 16
S5_STATE = 64
S5_CHUNK = 16
N_A = 2 * WIDTH + RWKV_IN + WIDTH
FFN_HIDDEN = 2816
WKV_CHUNK = 64
LANE_TILE = 128
SUBLANE_TILE = 8
QUAD = 4 * HEAD
VMEM_LIMIT = 56 * 1024 * 1024


def _bf(x):
    return x.astype(BF16)


def _mm(a, b):
    return jnp.dot(_bf(a), _bf(b), preferred_element_type=F32)


def _mm_nt(a, b):
    return lax.dot_general(_bf(a), _bf(b), (((1,), (1,)), ((), ())), preferred_element_type=F32)


def _mm_tn(a, b):
    return lax.dot_general(_bf(a), _bf(b), (((0,), (0,)), ((), ())), preferred_element_type=F32)


def _softplus(x):
    return jnp.maximum(x, 0.0) + jnp.log(1.0 + jnp.exp(-jnp.abs(x)))


def _head_sum(x, bd_ones):
    hi = _bf(x)
    lo = _bf(x - hi.astype(F32))
    return (jnp.dot(hi, bd_ones, preferred_element_type=F32)
            + jnp.dot(lo, bd_ones, preferred_element_type=F32))


def _mod_norm(x, g, shift, scale):
    y = x * lax.rsqrt(jnp.mean(x * x, axis=-1, keepdims=True) + EPS) * g
    return y * (1.0 + scale) + shift


def _params(n_axes=1):
    return pltpu.CompilerParams(dimension_semantics=("arbitrary",) * n_axes,
                                vmem_limit_bytes=VMEM_LIMIT)


def _layer_spec(arr, layer):
    rest = arr.shape[1:]
    return pl.BlockSpec((None,) + rest, lambda *_: (layer,) + (0,) * len(rest))


def _full_spec(arr):
    return pl.BlockSpec(arr.shape, lambda *_: (0,) * arr.ndim)


def _row_spec(tm, width, col=0):
    return pl.BlockSpec((tm, width), lambda i: (i, col))


def _cast_kernel(x_ref, *o_refs, splits):
    for o_ref, (lo, hi) in zip(o_refs, splits):
        o_ref[...] = _bf(x_ref[:, lo:hi])


def _cast_bf16(w, splits=None):
    lead, (rows, cols) = w.shape[:-2], w.shape[-2:]
    splits = splits or ((0, cols),)
    n_rows = math.prod(lead) * rows
    tm = 256
    outs = pl.pallas_call(
        functools.partial(_cast_kernel, splits=splits),
        grid=(n_rows // tm,),
        in_specs=[_row_spec(tm, cols)],
        out_specs=[_row_spec(tm, hi - lo) for lo, hi in splits],
        out_shape=[jax.ShapeDtypeStruct((n_rows, hi - lo), BF16) for lo, hi in splits],
        compiler_params=_params(),
        name="cast_bf16",
    )(w.reshape(n_rows, cols))
    return [o.reshape(lead + (rows, hi - lo)) for o, (lo, hi) in zip(outs, splits)]


def _mod_kernel(cond_ref, w_ref, b_ref, o_ref):
    cnd = cond_ref[...]
    act = cnd * jax.nn.sigmoid(cnd)
    o_ref[...] = jnp.dot(act, w_ref[...], preferred_element_type=F32, precision=HIGHEST) + b_ref[...]


def _modulation(cond, w_mod, b_mod):
    depth = w_mod.shape[0]
    n_col = w_mod.shape[2] // D
    return pl.pallas_call(
        _mod_kernel,
        grid=(depth, n_col),
        in_specs=[pl.BlockSpec((SUBLANE_TILE, D), lambda l, j: (0, 0)),
                  pl.BlockSpec((None, D, D), lambda l, j: (l, 0, j)),
                  pl.BlockSpec((None, 1, D), lambda l, j: (l, 0, j))],
        out_specs=pl.BlockSpec((None, SUBLANE_TILE, D), lambda l, j: (l, 0, j)),
        out_shape=jax.ShapeDtypeStruct((depth, SUBLANE_TILE, 6 * D), F32),
        compiler_params=_params(2),
        name="modulation",
    )(cond, w_mod, b_mod.reshape(depth, 1, 6 * D))


def _premix_kernel(h_ref, hp_ref, hn_ref, mod_ref, g_ref, w_ref, cw_ref, cb_ref, mu_ref, kk_ref, bd_ref,
                   xc_ref, ga_ref, zb_ref, uc_ref, kkn_ref, z_scr, *, row, tm):
    i = pl.program_id(0)
    n = pl.num_programs(0)
    halo = SUBLANE_TILE
    shift = mod_ref[row:row + 1, 0:D]
    scale = mod_ref[row:row + 1, D:2 * D]
    hext = jnp.concatenate([hp_ref[...], h_ref[...], hn_ref[...]], axis=0)
    xn = _mod_norm(hext, g_ref[...], shift, scale)
    z_scr[...] = _mm(xn, w_ref[...])

    @pl.when(i == 0)
    def _():
        z_scr[0:halo, :] = jnp.zeros((halo, N_A), F32)

    @pl.when(i == n - 1)
    def _():
        z_scr[tm + halo:tm + 2 * halo, :] = jnp.zeros((halo, N_A), F32)

    acc = cb_ref[...] + cw_ref[0:1, :] * z_scr[halo - 2:halo - 2 + tm, 0:WIDTH]
    for j in range(1, 4):
        acc = acc + cw_ref[j:j + 1, :] * z_scr[halo - 2 + j:halo - 2 + j + tm, 0:WIDTH]
    xc_ref[...] = acc
    ga_ref[...] = z_scr[halo:halo + tm, WIDTH:2 * WIDTH]
    lo, hi = 2 * WIDTH, 2 * WIDTH + RWKV_IN
    zc = z_scr[halo:halo + tm, lo:hi]
    zp = z_scr[halo - 1:halo - 1 + tm, lo:hi]
    zn = z_scr[halo + 1:halo + 1 + tm, lo:hi]
    zs = zc + mu_ref[...] * (0.5 * (zp + zn) - zc)
    zb_ref[...] = zs
    uc_ref[...] = z_scr[halo:halo + tm, hi:hi + WIDTH]
    kk = zs[:, WIDTH:2 * WIDTH] * kk_ref[...]
    ss = _head_sum(kk * kk, bd_ref[...])
    kkn_ref[...] = kk * lax.rsqrt(ss + 1e-12)


def _premix(h, layer, row, mod, norm1, w_a, conv_w, conv_b, mu, kk_w, bd_ones):
    n_tok = h.shape[0]
    tm = min(256, n_tok)
    n = n_tok // tm
    per = tm // SUBLANE_TILE
    last_blk = n_tok // SUBLANE_TILE - 1
    kern = functools.partial(_premix_kernel, row=row, tm=tm)
    outs = pl.pallas_call(
        kern,
        grid=(n,),
        in_specs=[
            _row_spec(tm, D),
            pl.BlockSpec((SUBLANE_TILE, D), lambda i: (jnp.maximum(i * per - 1, 0), 0)),
            pl.BlockSpec((SUBLANE_TILE, D), lambda i: (jnp.minimum((i + 1) * per, last_blk), 0)),
            _layer_spec(mod, layer), _layer_spec(norm1, layer), _layer_spec(w_a, layer),
            _layer_spec(conv_w, layer), _layer_spec(conv_b, layer), _layer_spec(mu, layer),
            _layer_spec(kk_w, layer), _full_spec(bd_ones),
        ],
        out_specs=[_row_spec(tm, WIDTH), _row_spec(tm, WIDTH), _row_spec(tm, RWKV_IN),
                   _row_spec(tm, WIDTH), _row_spec(tm, WIDTH)],
        out_shape=[jax.ShapeDtypeStruct((n_tok, WIDTH), F32), jax.ShapeDtypeStruct((n_tok, WIDTH), F32),
                   jax.ShapeDtypeStruct((n_tok, RWKV_IN), F32), jax.ShapeDtypeStruct((n_tok, WIDTH), F32),
                   jax.ShapeDtypeStruct((n_tok, WIDTH), F32)],
        scratch_shapes=[pltpu.VMEM((tm + 2 * SUBLANE_TILE, N_A), F32)],
        compiler_params=_params(),
        name="premix",
    )(h, h, h, mod, norm1, w_a, conv_w, conv_b, mu, kk_w, bd_ones)
    return outs


def _lru_kernel(xf_ref, xr_ref, wg_ref, bg_ref, lam_ref, h0_ref, hf_ref, hr_ref, fin_ref, carry, *, tm):
    i = pl.program_id(0)

    @pl.when(i == 0)
    def _():
        carry[...] = h0_ref[...]

    rows = lax.broadcasted_iota(jnp.int32, (tm, WIDTH), 0)
    for d, (x_ref, o_ref) in enumerate(((xf_ref, hf_ref), (xr_ref, hr_ref))):
        xc = x_ref[...]
        gates = _mm(xc, wg_ref[d]) + bg_ref[d]
        gate_r = jax.nn.sigmoid(gates[:, 0:WIDTH])
        gate_i = jax.nn.sigmoid(gates[:, WIDTH:2 * WIDTH])
        log_a = -LRU_C * gate_r * _softplus(-lam_ref[d])
        a = jnp.exp(log_a)
        b = jnp.sqrt(1.0 - jnp.exp(2.0 * log_a)) * (gate_i * xc)
        s = 1
        while s < tm:
            if d == 0:
                a_sh, b_sh, m = pltpu.roll(a, s, 0), pltpu.roll(b, s, 0), rows >= s
            else:
                a_sh, b_sh, m = pltpu.roll(a, tm - s, 0), pltpu.roll(b, tm - s, 0), rows < tm - s
            b = jnp.where(m, a * b_sh + b, b)
            a = jnp.where(m, a * a_sh, a)
            s *= 2
        hs = a * carry[d] + b
        o_ref[...] = hs
        carry[d] = hs[tm - 1:tm, :] if d == 0 else hs[0:1, :]
    fin_ref[...] = carry[...]


def _lru(xc, layer, w_gate, b_gate, lam, h0):
    n_tok = xc.shape[0]
    tm = min(256, n_tok)
    n = n_tok // tm
    kern = functools.partial(_lru_kernel, tm=tm)
    return pl.pallas_call(
        kern,
        grid=(n,),
        in_specs=[_row_spec(tm, WIDTH),
                  pl.BlockSpec((tm, WIDTH), lambda i: (n - 1 - i, 0)),
                  _layer_spec(w_gate, layer), _layer_spec(b_gate, layer), _layer_spec(lam, layer),
                  _full_spec(h0)],
        out_specs=[_row_spec(tm, WIDTH),
                   pl.BlockSpec((tm, WIDTH), lambda i: (n - 1 - i, 0)),
                   pl.BlockSpec((2, 1, WIDTH), lambda i: (0, 0, 0))],
        out_shape=[jax.ShapeDtypeStruct((n_tok, WIDTH), F32), jax.ShapeDtypeStruct((n_tok, WIDTH), F32),
                   jax.ShapeDtypeStruct((2, 1, WIDTH), F32)],
        scratch_shapes=[pltpu.VMEM((2, 1, WIDTH), F32)],
        compiler_params=_params(),
        name="rglru_scan",
    )(xc, xc, w_gate, b_gate, lam, h0)


def _block_rows(x, lane_head):
    return jnp.concatenate([jnp.where(lane_head == h, x, 0.0) for h in range(QUAD // HEAD)], axis=0)


def _interleave(gens):
    results = [None] * len(gens)
    active = list(range(len(gens)))
    while active:
        still = []
        for g in active:
            try:
                next(gens[g])
                still.append(g)
            except StopIteration as stop:
                results[g] = stop.value
        active = still
    return results


def _wkv_kernel(zf_ref, zr_ref, kf_ref, kr_ref, w2_ref, a2_ref, w0_ref, a0_ref, ka_ref, rk_ref, bd_ref, s0_ref,
                yf_ref, yr_ref, bf_ref, br_ref, sfin_ref, s_scr, *, n_chunk):
    i = pl.program_id(0)
    t = WKV_CHUNK

    @pl.when(i == 0)
    def _():
        s_scr[...] = s0_ref[...]

    row = lax.broadcasted_iota(jnp.int32, (t, QUAD), 0)
    lane = lax.broadcasted_iota(jnp.int32, (t, QUAD), 1)
    lane_head = lane >> 6
    lane_tok = lane & (HEAD - 1)
    row_t = lax.broadcasted_iota(jnp.int32, (t, t), 0)
    col_t = lax.broadcasted_iota(jnp.int32, (t, t), 1)
    same_head = (lax.broadcasted_iota(jnp.int32, (QUAD, QUAD), 0) >> 6) == (
        lax.broadcasted_iota(jnp.int32, (QUAD, QUAD), 1) >> 6)
    eye = jnp.where(lane_tok == row, 1.0, 0.0)
    pair_mask = [((row ^ lane_tok) >> lvl) == 1 for lvl in range(int(math.log2(t)))]

    n_quad = WIDTH // QUAD
    bd = functools.partial(_block_rows, lane_head=lane_head)

    def chunk_setup(d, at, rt, bt, kt, vq):
        strict, incl = (lane_tok < row, lane_tok <= row) if d == 0 else (lane_tok > row, lane_tok >= row)
        ar = jnp.concatenate([at, rt], axis=0)
        gb = _mm_nt(ar, bd(bt))
        yield
        gk = _mm_nt(ar, bd(kt))
        yield
        a_ab = jnp.where(strict, gb[0:t], 0.0)
        a_rb = jnp.where(incl, gb[t:2 * t], 0.0)
        a_ak = jnp.where(strict, gk[0:t], 0.0)
        a_rk = jnp.where(incl, gk[t:2 * t], 0.0)
        inv = eye + jnp.where(pair_mask[0], a_ab, 0.0)
        for lvl in range(1, len(pair_mask)):
            half = _mm(inv, bd(jnp.where(pair_mask[lvl], a_ab, 0.0)))
            yield
            inv = inv + _mm(half, bd(inv))
            yield
        v_bd = bd(vq)
        x0 = _mm(a_ak, v_bd)
        yield
        y0 = _mm(a_rk, v_bd)
        yield
        wu = _mm(inv, jnp.concatenate([bd(at), bd(x0)], axis=1))
        yield
        return wu[:, 0:QUAD], wu[:, QUAD:2 * QUAD], a_rb, y0

    def state_chain(d, q, order, pre, post, y_ref):
        s = s_scr[d, q]
        for c in order:
            w1, u0, a_rb, y0 = pre[(d, c, q)]
            rt, vq, b_end, k_end, decay = post[(d, c, q)]
            u = _mm_nt(w1, s) + u0
            yield
            y = _mm_nt(rt, s) + y0
            yield
            y = y + _mm(a_rb, bd(u))
            yield
            upd = _mm_tn(jnp.concatenate([u, vq], axis=0), jnp.concatenate([b_end, k_end], axis=0))
            yield
            s = s * decay + jnp.where(same_head, upd, 0.0)
            y_ref[c * t:(c + 1) * t, q * QUAD:(q + 1) * QUAD] = y
        s_scr[d, q] = s

    setups, post = {}, {}
    dirs = ((zf_ref, kf_ref, yf_ref, bf_ref), (zr_ref, kr_ref, yr_ref, br_ref))
    for d, (z_ref, k_ref, y_ref, b_ref) in enumerate(dirs):
        r = z_ref[:, 0:WIDTH]
        k = z_ref[:, WIDTH:2 * WIDTH]
        v = z_ref[:, 2 * WIDTH:3 * WIDTH]
        wd = z_ref[:, 3 * WIDTH:3 * WIDTH + LANE_TILE]
        ad = z_ref[:, 3 * WIDTH + LANE_TILE:3 * WIDTH + 2 * LANE_TILE]
        kkn = k_ref[...]
        w_log = -_softplus(-(w0_ref[d] + _mm(jnp.tanh(wd), w2_ref[d]))) - 0.5
        lw = -jnp.exp(w_log)
        iclr = jax.nn.sigmoid(a0_ref[d] + _mm(ad, a2_ref[d]))
        kd = k * (1.0 + (iclr - 1.0) * ka_ref[...])
        kb = kkn * iclr
        b_ref[...] = _head_sum(r * kd * rk_ref[...], bd_ref[...]) * v
        p1 = _bf(lw)
        p2 = _bf(lw - p1.astype(F32))
        lw2 = jnp.concatenate([p1, p2], axis=1)
        tri = jnp.where(col_t <= row_t if d == 0 else col_t >= row_t, 1.0, 0.0).astype(BF16)
        for c in range(n_chunk):
            rows_c = slice(c * t, (c + 1) * t)
            cum2 = jnp.dot(tri, lw2[rows_c], preferred_element_type=F32)
            cum = cum2[:, 0:WIDTH] + cum2[:, WIDTH:2 * WIDTH]
            tot = cum[t - 1:t, :] if d == 0 else cum[0:1, :]
            e_neg = jnp.exp(-cum)
            e_end = jnp.exp(tot - cum)
            a_t = -kkn[rows_c] * jnp.exp(cum - lw[rows_c])
            r_t = r[rows_c] * jnp.exp(cum)
            b_t = kb[rows_c] * e_neg
            k_t = kd[rows_c] * e_neg
            b_end = kb[rows_c] * e_end
            k_end = kd[rows_c] * e_end
            decay_tot = jnp.exp(tot)
            for q in range(n_quad):
                sl = slice(q * QUAD, (q + 1) * QUAD)
                vq = v[rows_c, sl]
                setups[(d, c, q)] = chunk_setup(d, a_t[:, sl], r_t[:, sl], b_t[:, sl], k_t[:, sl], vq)
                post[(d, c, q)] = (r_t[:, sl], vq, b_end[:, sl], k_end[:, sl], decay_tot[:, sl])

    keys = list(setups)
    pre = dict(zip(keys, _interleave([setups[key] for key in keys])))
    chains = []
    for d, y_ref in ((0, yf_ref), (1, yr_ref)):
        order = range(n_chunk) if d == 0 else range(n_chunk - 1, -1, -1)
        chains += [state_chain(d, q, order, pre, post, y_ref) for q in range(n_quad)]
    _interleave(chains)
    sfin_ref[...] = s_scr[...]


def _wkv(zbs, kkn, layer, w2p, a2p, w0, a0, ka, rk, bd_ones, s0):
    n_tok = zbs.shape[0]
    t = min(256, n_tok)
    n = n_tok // t
    fwd = lambda i: (i, 0)
    rev = lambda i: (n - 1 - i, 0)
    state_shape = s0.shape
    return pl.pallas_call(
        functools.partial(_wkv_kernel, n_chunk=t // WKV_CHUNK),
        grid=(n,),
        in_specs=[pl.BlockSpec((t, RWKV_IN), fwd), pl.BlockSpec((t, RWKV_IN), rev),
                  pl.BlockSpec((t, WIDTH), fwd), pl.BlockSpec((t, WIDTH), rev),
                  _layer_spec(w2p, layer), _layer_spec(a2p, layer), _layer_spec(w0, layer),
                  _layer_spec(a0, layer), _layer_spec(ka, layer), _layer_spec(rk, layer),
                  _full_spec(bd_ones), _full_spec(s0)],
        out_specs=[pl.BlockSpec((t, WIDTH), fwd), pl.BlockSpec((t, WIDTH), rev),
                   pl.BlockSpec((t, WIDTH), fwd), pl.BlockSpec((t, WIDTH), rev),
                   pl.BlockSpec(state_shape, lambda i: (0, 0, 0, 0))],
        out_shape=[jax.ShapeDtypeStruct((n_tok, WIDTH), F32)] * 4 + [jax.ShapeDtypeStruct(state_shape, F32)],
        scratch_shapes=[pltpu.VMEM(state_shape, F32)],
        compiler_params=_params(),
        name="wkv7_scan",
    )(zbs, zbs, kkn, kkn, w2p, a2p, w0, a0, ka, rk, bd_ones, s0)


def _cmul_rows(x, p):
    half = LANE_TILE // 2
    lane = lax.broadcasted_iota(jnp.int32, p.shape, 1)
    p_sw = pltpu.roll(p, half, 1)
    p1 = jnp.where(lane < half, p, p_sw)[0:1, :]
    p2 = jnp.where(lane < half, -p_sw, p)[0:1, :]
    return x * p1 + pltpu.roll(x, half, 1) * p2


def _s5_kernel(u_ref, w_ref, v_ref, lam_ref, h0_ref, y_ref, fin_ref, *, nc):
    blk = S5_CHUNK * S5_GROUP
    res = _mm(u_ref[...], w_ref[...])
    row = lax.broadcasted_iota(jnp.int32, (nc, LANE_TILE), 0)
    h_in = []
    for d in range(2):
        e = res[:, blk + d * LANE_TILE:blk + (d + 1) * LANE_TILE]
        lam = jnp.broadcast_to(lam_ref[d], (SUBLANE_TILE, LANE_TILE))
        h0 = jnp.broadcast_to(h0_ref[d], (SUBLANE_TILE, LANE_TILE))
        first = 0 if d == 0 else nc - 1
        e = jnp.where(row == first, e + _cmul_rows(h0, lam)[0:1, :], e)
        p = lam
        s = 1
        while s < nc:
            if d == 0:
                sh, m = pltpu.roll(e, s, 0), row >= s
            else:
                sh, m = pltpu.roll(e, nc - s, 0), row < nc - s
            e = e + jnp.where(m, _cmul_rows(sh, p), 0.0)
            p = _cmul_rows(p, p)
            s *= 2
        if d == 0:
            fin_ref[d] = e[nc - 1:nc, :]
            h_in.append(jnp.where(row >= 1, pltpu.roll(e, 1, 0), h0[0:1, :]))
        else:
            fin_ref[d] = e[0:1, :]
            h_in.append(jnp.where(row < nc - 1, pltpu.roll(e, nc - 1, 0), h0[0:1, :]))
    y_ref[...] = res[:, 0:blk] + _mm(jnp.concatenate(h_in, axis=1), v_ref[...])


def _s5(u_blocks, layer, w_cat, v_cat, lam16, h0):
    groups, nc, blk = u_blocks.shape
    kern = functools.partial(_s5_kernel, nc=nc)
    return pl.pallas_call(
        kern,
        grid=(groups,),
        in_specs=[pl.BlockSpec((None, nc, blk), lambda g: (g, 0, 0)),
                  pl.BlockSpec((None, None, blk, w_cat.shape[-1]), lambda g: (layer, g, 0, 0)),
                  pl.BlockSpec((None, None, blk, blk), lambda g: (layer, g, 0, 0)),
                  pl.BlockSpec((None, 2, None, 1, LANE_TILE), lambda g: (layer, 0, g, 0, 0)),
                  pl.BlockSpec((2, None, 1, LANE_TILE), lambda g: (0, g, 0, 0))],
        out_specs=[pl.BlockSpec((None, nc, blk), lambda g: (g, 0, 0)),
                   pl.BlockSpec((2, None, 1, LANE_TILE), lambda g: (0, g, 0, 0))],
        out_shape=[jax.ShapeDtypeStruct((groups, nc, blk), F32),
                   jax.ShapeDtypeStruct((2, groups, 1, LANE_TILE), F32)],
        compiler_params=_params(),
        name="s5_scan",
    )(u_blocks, w_cat, v_cat, lam16, h0)


def _s5_weights(lam_re, lam_im, log_step, b_re, b_im, c_re, c_im):
    n = S5_CHUNK
    step = jnp.exp(log_step)[..., None]
    x_re, ang = lam_re * step, lam_im * step
    mag = jnp.exp(x_re)
    lb_re, lb_im = mag * jnp.cos(ang), mag * jnp.sin(ang)
    nr = jnp.expm1(x_re) * jnp.cos(ang) - 2.0 * jnp.square(jnp.sin(0.5 * ang))
    den = lam_re * lam_re + lam_im * lam_im
    f_re = (nr * lam_re + lb_im * lam_im) / den
    f_im = (lb_im * lam_re - nr * lam_im) / den
    bb_re = f_re[..., None] * b_re - f_im[..., None] * b_im
    bb_im = f_re[..., None] * b_im + f_im[..., None] * b_re
    pr, pi = [jnp.ones_like(lb_re)], [jnp.zeros_like(lb_re)]
    for _ in range(n):
        pr, pi = pr + [pr[-1] * lb_re - pi[-1] * lb_im], pi + [pr[-1] * lb_im + pi[-1] * lb_re]
    pw_re, pw_im = jnp.stack(pr, axis=-2), jnp.stack(pi, axis=-2)
    cp_re = c_re[..., None, :, :] * pw_re[..., :, None, :] - c_im[..., None, :, :] * pw_im[..., :, None, :]
    cp_im = c_re[..., None, :, :] * pw_im[..., :, None, :] + c_im[..., None, :, :] * pw_re[..., :, None, :]
    taps = (jnp.einsum('...jcp,...pk->...jck', cp_re, bb_re, precision=HIGHEST)
            - jnp.einsum('...jcp,...pk->...jck', cp_im, bb_im, precision=HIGHEST))
    s_idx = jnp.arange(n)[:, None]
    t_idx = jnp.arange(n)[None, :]

    def toeplitz(k, lag):
        m = jnp.take(k, jnp.clip(lag, 0, n).reshape(-1), axis=-3)
        m = m.reshape(k.shape[:-3] + (n, n, S5_GROUP, S5_GROUP))
        m = jnp.where((lag >= 0)[:, :, None, None], m, 0.0)
        m = jnp.swapaxes(jnp.swapaxes(m, -1, -2), -2, -3)
        return m.reshape(k.shape[:-3] + (n * S5_GROUP, n * S5_GROUP))

    m_tot = toeplitz(taps[:, 0], t_idx - s_idx) + toeplitz(taps[:, 1], s_idx - t_idx)

    def end_state(d, order):
        qr, qi = pw_re[:, d][..., order, :], pw_im[:, d][..., order, :]
        er = qr[..., :, None, :] * jnp.swapaxes(bb_re[:, d], -1, -2)[..., None, :, :] \
            - qi[..., :, None, :] * jnp.swapaxes(bb_im[:, d], -1, -2)[..., None, :, :]
        ei = qr[..., :, None, :] * jnp.swapaxes(bb_im[:, d], -1, -2)[..., None, :, :] \
            + qi[..., :, None, :] * jnp.swapaxes(bb_re[:, d], -1, -2)[..., None, :, :]
        e = jnp.concatenate([er, ei], axis=-1)
        return e.reshape(e.shape[:-3] + (n * S5_GROUP, 2 * S5_STATE))

    def state_out(d, order):
        vr = jnp.moveaxis(cp_re[:, d][..., order, :, :], -1, -3)
        vi = -jnp.moveaxis(cp_im[:, d][..., order, :, :], -1, -3)
        vv = jnp.concatenate([vr, vi], axis=-3)
        return vv.reshape(vv.shape[:-3] + (2 * S5_STATE, n * S5_GROUP))

    asc = jnp.arange(n)
    w_cat = jnp.concatenate([m_tot, end_state(0, n - 1 - asc), end_state(1, asc)], axis=-1)
    v_cat = jnp.concatenate([state_out(0, asc + 1), state_out(1, n - asc)], axis=-2)
    lam16 = jnp.concatenate([pw_re[..., n, :], pw_im[..., n, :]], axis=-1)[..., None, :]
    return _bf(w_cat), _bf(v_cat), lam16


def _merge_kernel(h_ref, hf_ref, hr_ref, ga_ref, yf_ref, yr_ref, bf_ref, br_ref, gd_ref, ys_ref, uc_ref,
                  mod_ref, g_ref, wzg_ref, lnw_ref, lnb_ref, g2_ref, dsk_ref, wglu_ref, bglu_ref,
                  wbr_ref, wout_ref, bd_ref, o_ref, *, row):
    h = h_ref[...]
    shift = mod_ref[row:row + 1, 0:D]
    scale = mod_ref[row:row + 1, D:2 * D]
    gate = mod_ref[row:row + 1, 2 * D:3 * D]
    zg = _mm(_mod_norm(h, g_ref[...], shift, scale), wzg_ref[...])
    bd = bd_ref[...]
    y_a = jax.nn.gelu(ga_ref[...]) * (hf_ref[...] + hr_ref[...])
    y = yf_ref[...] + yr_ref[...]
    inv_n = 1.0 / HEAD
    yc = y - _head_sum(y, bd) * inv_n
    var = _head_sum(yc * yc, bd) * inv_n
    y_b = yc * lax.rsqrt(var + GN_EPS) * lnw_ref[...] + lnb_ref[...] + (bf_ref[...] + br_ref[...])
    y_b = y_b * _mm(jax.nn.sigmoid(gd_ref[...]), g2_ref[...])
    y_s = jax.nn.gelu(ys_ref[...] + dsk_ref[...] * uc_ref[...])
    y_c = y_s * jax.nn.sigmoid(_mm(y_s, wglu_ref[...]) + bglu_ref[...])
    mix = (jax.nn.sigmoid(zg[:, 0:D]) * _mm(y_a, wbr_ref[0])
           + jax.nn.sigmoid(zg[:, D:2 * D]) * _mm(y_b, wbr_ref[1])
           + jax.nn.sigmoid(zg[:, 2 * D:3 * D]) * _mm(y_c, wbr_ref[2]))
    o_ref[...] = h + gate * _mm(mix, wout_ref[...])


def _merge(h, parts, layer, row, mod, norm1, w_zg, lnw, lnb, g2, dsk, w_glu, b_glu, w_branch, w_out, bd_ones):
    n_tok = h.shape[0]
    tm = min(256, n_tok)
    hf, hr, ga, yf, yr, bf_, br_, zbs, ys5, uc = parts
    kern = functools.partial(_merge_kernel, row=row)
    rs = _row_spec(tm, WIDTH)
    gd_col = (3 * WIDTH + 2 * LANE_TILE) // LANE_TILE
    weights = (mod, norm1, w_zg, lnw, lnb, g2, dsk, w_glu, b_glu, w_branch, w_out)
    return pl.pallas_call(
        kern,
        grid=(n_tok // tm,),
        in_specs=[_row_spec(tm, D), rs, rs, rs, rs, rs, rs, rs, _row_spec(tm, LANE_TILE, gd_col), rs, rs]
        + [_layer_spec(w, layer) for w in weights] + [_full_spec(bd_ones)],
        out_specs=_row_spec(tm, D),
        out_shape=jax.ShapeDtypeStruct((n_tok, D), F32),
        compiler_params=_params(),
        name="merge_project",
    )(h, hf, hr, ga, yf, yr, bf_, br_, zbs, ys5, uc, *weights, bd_ones)


def _ffn_kernel(h_ref, mod_ref, g_ref, win_ref, wout_ref, o_ref, *, row):
    h = h_ref[...]
    shift = mod_ref[row:row + 1, 3 * D:4 * D]
    scale = mod_ref[row:row + 1, 4 * D:5 * D]
    gate = mod_ref[row:row + 1, 5 * D:6 * D]
    gu = _mm(_mod_norm(h, g_ref[...], shift, scale), win_ref[...])
    a, up = gu[:, 0:FFN_HIDDEN], gu[:, FFN_HIDDEN:2 * FFN_HIDDEN]
    o_ref[...] = h + gate * _mm(a * jax.nn.sigmoid(a) * up, wout_ref[...])


def _ffn(h, layer, row, mod, norm2, w_in, w_out):
    n_tok = h.shape[0]
    tm = min(256, n_tok)
    kern = functools.partial(_ffn_kernel, row=row)
    return pl.pallas_call(
        kern,
        grid=(n_tok // tm,),
        in_specs=[_row_spec(tm, D), _layer_spec(mod, layer), _layer_spec(norm2, layer),
                  _layer_spec(w_in, layer), _layer_spec(w_out, layer)],
        out_specs=_row_spec(tm, D),
        out_shape=jax.ShapeDtypeStruct((n_tok, D), F32),
        compiler_params=_params(),
        name="swiglu",
    )(h, mod, norm2, w_in, w_out)


def _final_norm_kernel(h_ref, g_ref, o_ref):
    x = h_ref[...]
    o_ref[...] = x * lax.rsqrt(jnp.mean(x * x, axis=-1, keepdims=True) + EPS) * g_ref[...]


def _final_norm(h, g):
    n_tok = h.shape[0]
    tm = min(512, n_tok)
    return pl.pallas_call(
        _final_norm_kernel,
        grid=(n_tok // tm,),
        in_specs=[_row_spec(tm, D), _full_spec(g)],
        out_specs=_row_spec(tm, D),
        out_shape=jax.ShapeDtypeStruct((n_tok, D), F32),
        compiler_params=_params(),
        name="final_norm",
    )(h, g)


def _block_diag(w):
    n_h, a, b = w.shape[-3:]
    out = w[..., :, :, None, :] * jnp.eye(n_h, dtype=w.dtype)[:, None, :, None]
    return out.reshape(w.shape[:-3] + (n_h * a, n_h * b))


def _pad_lora(w):
    z = jnp.zeros_like(w[:, 0])
    return jnp.stack([jnp.concatenate([w[:, 0], z], axis=1), jnp.concatenate([z, w[:, 1]], axis=1)], axis=1)


def _to_s5_blocks(u, col_major):
    n_tok = u.shape[0]
    if col_major:
        rows = n_tok // GRID_W
        x = u.reshape(rows // S5_CHUNK, S5_CHUNK, GRID_W, S5_GROUPS, S5_GROUP)
        x = x.transpose(3, 2, 0, 1, 4)
    else:
        x = u.reshape(n_tok // S5_CHUNK, S5_CHUNK, S5_GROUPS, S5_GROUP).transpose(2, 0, 1, 3)
    return x.reshape(S5_GROUPS, n_tok // S5_CHUNK, S5_CHUNK * S5_GROUP)


def _from_s5_blocks(y, n_tok, col_major):
    if col_major:
        rows = n_tok // GRID_W
        x = y.reshape(S5_GROUPS, GRID_W, rows // S5_CHUNK, S5_CHUNK, S5_GROUP).transpose(2, 3, 1, 0, 4)
    else:
        x = y.reshape(S5_GROUPS, n_tok // S5_CHUNK, S5_CHUNK, S5_GROUP).transpose(1, 2, 0, 3)
    return x.reshape(n_tok, WIDTH)


def _prepare(c, c_ctx, w_mod, b_mod, norm1, norm2, w_in, lru_conv_w, lru_conv_b, lru_wa, lru_ba, lru_wx, lru_bx,
             lru_lam, rwkv_mu, rwkv_w0, rwkv_w2, rwkv_a0, rwkv_a2, rwkv_g2, rwkv_kk, rwkv_ka, rwkv_rk, rwkv_lnw,
             rwkv_lnb, s5_lam_re, s5_lam_im, s5_log_step, s5_b_re, s5_b_im, s5_c_re, s5_c_im, s5_d, s5_w_glu,
             s5_b_glu, w_branch, w_out, w_ffn_in, w_ffn_out):
    depth = w_in.shape[0]
    vec = lambda a: a.reshape(depth, 1, a.shape[-1])
    cond = jnp.concatenate([c, c_ctx[None], jnp.zeros((SUBLANE_TILE - 2, D), F32)], axis=0)
    w_a, w_zg = _cast_bf16(w_in, ((0, N_A), (N_A, w_in.shape[-1])))
    s5_w, s5_v, s5_lam16 = _s5_weights(s5_lam_re, s5_lam_im, s5_log_step, s5_b_re, s5_b_im, s5_c_re, s5_c_im)
    head_id = jnp.arange(WIDTH) // HEAD
    return dict(
        mod=_modulation(cond, w_mod, b_mod),
        w_a=w_a, w_zg=w_zg,
        w_gate=_bf(jnp.concatenate([_block_diag(lru_wa), _block_diag(lru_wx)], axis=-1)),
        b_gate=jnp.concatenate([lru_ba, lru_bx], axis=-1)[:, :, None, :],
        lam=lru_lam[:, :, None, :],
        w2p=_bf(_pad_lora(rwkv_w2)), a2p=_bf(_pad_lora(rwkv_a2)),
        w0=rwkv_w0[:, :, None, :], a0=rwkv_a0[:, :, None, :], rk=rwkv_rk.reshape(depth, 1, WIDTH),
        s5_w=s5_w, s5_v=s5_v, s5_lam16=s5_lam16,
        w_branch=_cast_bf16(w_branch)[0], w_out=_cast_bf16(w_out)[0],
        w_ffn_in=_cast_bf16(w_ffn_in)[0], w_ffn_out=_cast_bf16(w_ffn_out)[0],
        g2=_bf(rwkv_g2), w_glu=_bf(s5_w_glu), n1=vec(norm1), n2=vec(norm2), conv_w=lru_conv_w,
        conv_b=vec(lru_conv_b), mu=vec(rwkv_mu), kk_w=vec(rwkv_kk), ka=vec(rwkv_ka), lnw=vec(rwkv_lnw),
        lnb=vec(rwkv_lnb), dsk=vec(s5_d), b_glu=vec(s5_b_glu),
        bd_ones=(head_id[:, None] == head_id[None, :]).astype(BF16),
    )


def _zero_states():
    return (jnp.zeros((2, 1, WIDTH), F32),
            jnp.zeros((2, WIDTH // QUAD, QUAD, QUAD), F32),
            jnp.zeros((2, S5_GROUPS, 1, LANE_TILE), F32))


def _mixers(p, h, layer, row, states, col_major):
    n_tok = h.shape[0]
    xc, ga, zbs, uc, kkn = _premix(h, layer, row, p["mod"], p["n1"], p["w_a"], p["conv_w"], p["conv_b"],
                                   p["mu"], p["kk_w"], p["bd_ones"])
    hf, hr, fin_lru = _lru(xc, layer, p["w_gate"], p["b_gate"], p["lam"], states[0])
    yf, yr, bf_, br_, fin_wkv = _wkv(zbs, kkn, layer, p["w2p"], p["a2p"], p["w0"], p["a0"], p["ka"], p["rk"],
                                     p["bd_ones"], states[1])
    ys5, fin_s5 = _s5(_to_s5_blocks(uc, col_major), layer, p["s5_w"], p["s5_v"], p["s5_lam16"], states[2])
    ys5 = _from_s5_blocks(ys5, n_tok, col_major)
    return (hf, hr, ga, yf, yr, bf_, br_, zbs, ys5, uc), (fin_lru, fin_wkv, fin_s5)


def _channel_mix(p, h, parts, layer, row):
    h = _merge(h, parts, layer, row, p["mod"], p["n1"], p["w_zg"], p["lnw"], p["lnb"], p["g2"], p["dsk"],
               p["w_glu"], p["b_glu"], p["w_branch"], p["w_out"], p["bd_ones"])
    return _ffn(h, layer, row, p["mod"], p["n2"], p["w_ffn_in"], p["w_ffn_out"])


def kernel(x, c, ctx, c_ctx, w_mod, b_mod, norm1, norm2, norm_f, w_in, lru_conv_w, lru_conv_b, lru_wa, lru_ba,
           lru_wx, lru_bx, lru_lam, rwkv_mu, rwkv_w0, rwkv_w2, rwkv_a0, rwkv_a2, rwkv_g2, rwkv_kk, rwkv_ka,
           rwkv_rk, rwkv_lnw, rwkv_lnb, s5_lam_re, s5_lam_im, s5_log_step, s5_b_re, s5_b_im, s5_c_re, s5_c_im,
           s5_d, s5_w_glu, s5_b_glu, w_branch, w_out, w_ffn_in, w_ffn_out):
    bsz, n_lat, d_model = x.shape
    n_ctx = ctx.shape[1]
    depth = w_in.shape[0]
    assert bsz == 1 and d_model == D
    assert n_lat % (GRID_W * S5_CHUNK) == 0 and n_lat % 256 == 0
    assert n_ctx % WKV_CHUNK == 0 and (n_ctx <= 256 or n_ctx % 256 == 0)
    p = _prepare(c, c_ctx, w_mod, b_mod, norm1, norm2, w_in, lru_conv_w, lru_conv_b, lru_wa, lru_ba, lru_wx,
                 lru_bx, lru_lam, rwkv_mu, rwkv_w0, rwkv_w2, rwkv_a0, rwkv_a2, rwkv_g2, rwkv_kk, rwkv_ka, rwkv_rk,
                 rwkv_lnw, rwkv_lnb, s5_lam_re, s5_lam_im, s5_log_step, s5_b_re, s5_b_im, s5_c_re, s5_c_im, s5_d,
                 s5_w_glu, s5_b_glu, w_branch, w_out, w_ffn_in, w_ffn_out)
    h_lat, h_ctx = x[0], ctx[0]
    for layer in range(depth):
        parts_c, states = _mixers(p, h_ctx, layer, 1, _zero_states(), False)
        parts_l, _ = _mixers(p, h_lat, layer, 0, states, True)
        h_lat = _channel_mix(p, h_lat, parts_l, layer, 0)
        if layer != depth - 1:
            h_ctx = _channel_mix(p, h_ctx, parts_c, layer, 1)
    return _final_norm(h_lat, norm_f.reshape(1, D))[None]
```

```python
import functools
import math

import jax
import jax.numpy as jnp
from jax import lax
from jax.experimental import pallas as pl
from jax.experimental.pallas import tpu as pltpu

F32 = jnp.float32
BF16 = jnp.bfloat16
HIGHEST = lax.Precision.HIGHEST

D = 1024
WIDTH = 512
GRID_W = 64
EPS = 1e-6
LRU_C = 8.0
HEAD = 64
RWKV_IN = 3 * WIDTH + 2 * 64 + 2 * 64 + 128
GN_EPS = 64e-5
S5_GROUPS = 32
S5_GROUP = 16
S5_STATE = 64
S5_CHUNK = 16
N_A = 2 * WIDTH + RWKV_IN + WIDTH
FFN_HIDDEN = 2816
WKV_CHUNK = 64
LANE_TILE = 128
SUBLANE_TILE = 8
QUAD = 4 * HEAD
VMEM_LIMIT = 56 * 1024 * 1024


def _bf(x):
    return x.astype(BF16)


def _mm(a, b):
    return jnp.dot(_bf(a), _bf(b), preferred_element_type=F32)


def _mm_nt(a, b):
    return lax.dot_general(_bf(a), _bf(b), (((1,), (1,)), ((), ())), preferred_element_type=F32)


def _mm_tn(a, b):
    return lax.dot_general(_bf(a), _bf(b), (((0,), (0,)), ((), ())), preferred_element_type=F32)


def _softplus(x):
    return jnp.maximum(x, 0.0) + jnp.log(1.0 + jnp.exp(-jnp.abs(x)))


def _head_sum(x, bd_ones):
    hi = _bf(x)
    lo = _bf(x - hi.astype(F32))
    return (jnp.dot(hi, bd_ones, preferred_element_type=F32)
            + jnp.dot(lo, bd_ones, preferred_element_type=F32))


def _mod_norm(x, g, shift, scale):
    y = x * lax.rsqrt(jnp.mean(x * x, axis=-1, keepdims=True) + EPS) * g
    return y * (1.0 + scale) + shift


def _params(n_axes=1):
    return pltpu.CompilerParams(dimension_semantics=("arbitrary",) * n_axes,
                                vmem_limit_bytes=VMEM_LIMIT)


def _layer_spec(arr, layer):
    rest = arr.shape[1:]
    return pl.BlockSpec((None,) + rest, lambda *_: (layer,) + (0,) * len(rest))


def _full_spec(arr):
    return pl.BlockSpec(arr.shape, lambda *_: (0,) * arr.ndim)


def _row_spec(tm, width, col=0):
    return pl.BlockSpec((tm, width), lambda i: (i, col))


def _cast_kernel(x_ref, *o_refs, splits):
    for o_ref, (lo, hi) in zip(o_refs, splits):
        o_ref[...] = _bf(x_ref[:, lo:hi])


def _cast_bf16(w, splits=None):
    lead, (rows, cols) = w.shape[:-2], w.shape[-2:]
    splits = splits or ((0, cols),)
    n_rows = math.prod(lead) * rows
    tm = 256
    outs = pl.pallas_call(
        functools.partial(_cast_kernel, splits=splits),
        grid=(n_rows // tm,),
        in_specs=[_row_spec(tm, cols)],
        out_specs=[_row_spec(tm, hi - lo) for lo, hi in splits],
        out_shape=[jax.ShapeDtypeStruct((n_rows, hi - lo), BF16) for lo, hi in splits],
        compiler_params=_params(),
        name="cast_bf16",
    )(w.reshape(n_rows, cols))
    return [o.reshape(lead + (rows, hi - lo)) for o, (lo, hi) in zip(outs, splits)]


def _mod_kernel(cond_ref, w_ref, b_ref, o_ref):
    cnd = cond_ref[...]
    act = cnd * jax.nn.sigmoid(cnd)
    o_ref[...] = jnp.dot(act, w_ref[...], preferred_element_type=F32, precision=HIGHEST) + b_ref[...]


def _modulation(cond, w_mod, b_mod):
    depth = w_mod.shape[0]
    n_col = w_mod.shape[2] // D
    return pl.pallas_call(
        _mod_kernel,
        grid=(depth, n_col),
        in_specs=[pl.BlockSpec((SUBLANE_TILE, D), lambda l, j: (0, 0)),
                  pl.BlockSpec((None, D, D), lambda l, j: (l, 0, j)),
                  pl.BlockSpec((None, 1, D), lambda l, j: (l, 0, j))],
        out_specs=pl.BlockSpec((None, SUBLANE_TILE, D), lambda l, j: (l, 0, j)),
        out_shape=jax.ShapeDtypeStruct((depth, SUBLANE_TILE, 6 * D), F32),
        compiler_params=_params(2),
        name="modulation",
    )(cond, w_mod, b_mod.reshape(depth, 1, 6 * D))


def _premix_kernel(h_ref, hp_ref, hn_ref, mod_ref, g_ref, w_ref, cw_ref, cb_ref, mu_ref, kk_ref, bd_ref,
                   xc_ref, ga_ref, zb_ref, uc_ref, kkn_ref, z_scr, *, row, tm):
    i = pl.program_id(0)
    n = pl.num_programs(0)
    halo = SUBLANE_TILE
    shift = mod_ref[row:row + 1, 0:D]
    scale = mod_ref[row:row + 1, D:2 * D]
    hext = jnp.concatenate([hp_ref[...], h_ref[...], hn_ref[...]], axis=0)
    xn = _mod_norm(hext, g_ref[...], shift, scale)
    z_scr[...] = _mm(xn, w_ref[...])

    @pl.when(i == 0)
    def _():
        z_scr[0:halo, :] = jnp.zeros((halo, N_A), F32)

    @pl.when(i == n - 1)
    def _():
        z_scr[tm + halo:tm + 2 * halo, :] = jnp.zeros((halo, N_A), F32)

    acc = cb_ref[...] + cw_ref[0:1, :] * z_scr[halo - 2:halo - 2 + tm, 0:WIDTH]
    for j in range(1, 4):
        acc = acc + cw_ref[j:j + 1, :] * z_scr[halo - 2 + j:halo - 2 + j + tm, 0:WIDTH]
    xc_ref[...] = acc
    ga_ref[...] = z_scr[halo:halo + tm, WIDTH:2 * WIDTH]
    lo, hi = 2 * WIDTH, 2 * WIDTH + RWKV_IN
    zc = z_scr[halo:halo + tm, lo:hi]
    zp = z_scr[halo - 1:halo - 1 + tm, lo:hi]
    zn = z_scr[halo + 1:halo + 1 + tm, lo:hi]
    zs = zc + mu_ref[...] * (0.5 * (zp + zn) - zc)
    zb_ref[...] = zs
    uc_ref[...] = z_scr[halo:halo + tm, hi:hi + WIDTH]
    kk = zs[:, WIDTH:2 * WIDTH] * kk_ref[...]
    ss = _head_sum(kk * kk, bd_ref[...])
    kkn_ref[...] = kk * lax.rsqrt(ss + 1e-12)


def _premix(h, layer, row, mod, norm1, w_a, conv_w, conv_b, mu, kk_w, bd_ones):
    n_tok = h.shape[0]
    tm = min(256, n_tok)
    n = n_tok // tm
    per = tm // SUBLANE_TILE
    last_blk = n_tok // SUBLANE_TILE - 1
    kern = functools.partial(_premix_kernel, row=row, tm=tm)
    outs = pl.pallas_call(
        kern,
        grid=(n,),
        in_specs=[
            _row_spec(tm, D),
            pl.BlockSpec((SUBLANE_TILE, D), lambda i: (jnp.maximum(i * per - 1, 0), 0)),
            pl.BlockSpec((SUBLANE_TILE, D), lambda i: (jnp.minimum((i + 1) * per, last_blk), 0)),
            _layer_spec(mod, layer), _layer_spec(norm1, layer), _layer_spec(w_a, layer),
            _layer_spec(conv_w, layer), _layer_spec(conv_b, layer), _layer_spec(mu, layer),
            _layer_spec(kk_w, layer), _full_spec(bd_ones),
        ],
        out_specs=[_row_spec(tm, WIDTH), _row_spec(tm, WIDTH), _row_spec(tm, RWKV_IN),
                   _row_spec(tm, WIDTH), _row_spec(tm, WIDTH)],
        out_shape=[jax.ShapeDtypeStruct((n_tok, WIDTH), F32), jax.ShapeDtypeStruct((n_tok, WIDTH), F32),
                   jax.ShapeDtypeStruct((n_tok, RWKV_IN), F32), jax.ShapeDtypeStruct((n_tok, WIDTH), F32),
                   jax.ShapeDtypeStruct((n_tok, WIDTH), F32)],
        scratch_shapes=[pltpu.VMEM((tm + 2 * SUBLANE_TILE, N_A), F32)],
        compiler_params=_params(),
        name="premix",
    )(h, h, h, mod, norm1, w_a, conv_w, conv_b, mu, kk_w, bd_ones)
    return outs


def _lru_kernel(xf_ref, xr_ref, wg_ref, bg_ref, lam_ref, h0_ref, hf_ref, hr_ref, fin_ref, carry, *, tm):
    i = pl.program_id(0)

    @pl.when(i == 0)
    def _():
        carry[...] = h0_ref[...]

    rows = lax.broadcasted_iota(jnp.int32, (tm, WIDTH), 0)
    for d, (x_ref, o_ref) in enumerate(((xf_ref, hf_ref), (xr_ref, hr_ref))):
        xc = x_ref[...]
        gates = _mm(xc, wg_ref[d]) + bg_ref[d]
        gate_r = jax.nn.sigmoid(gates[:, 0:WIDTH])
        gate_i = jax.nn.sigmoid(gates[:, WIDTH:2 * WIDTH])
        log_a = -LRU_C * gate_r * _softplus(-lam_ref[d])
        a = jnp.exp(log_a)
        b = jnp.sqrt(1.0 - jnp.exp(2.0 * log_a)) * (gate_i * xc)
        s = 1
        while s < tm:
            if d == 0:
                a_sh, b_sh, m = pltpu.roll(a, s, 0), pltpu.roll(b, s, 0), rows >= s
            else:
                a_sh, b_sh, m = pltpu.roll(a, tm - s, 0), pltpu.roll(b, tm - s, 0), rows < tm - s
            b = jnp.where(m, a * b_sh + b, b)
            a = jnp.where(m, a * a_sh, a)
            s *= 2
        hs = a * carry[d] + b
        o_ref[...] = hs
        carry[d] = hs[tm - 1:tm, :] if d == 0 else hs[0:1, :]
    fin_ref[...] = carry[...]


def _lru(xc, layer, w_gate, b_gate, lam, h0):
    n_tok = xc.shape[0]
    tm = min(256, n_tok)
    n = n_tok // tm
    kern = functools.partial(_lru_kernel, tm=tm)
    return pl.pallas_call(
        kern,
        grid=(n,),
        in_specs=[_row_spec(tm, WIDTH),
                  pl.BlockSpec((tm, WIDTH), lambda i: (n - 1 - i, 0)),
                  _layer_spec(w_gate, layer), _layer_spec(b_gate, layer), _layer_spec(lam, layer),
                  _full_spec(h0)],
        out_specs=[_row_spec(tm, WIDTH),
                   pl.BlockSpec((tm, WIDTH), lambda i: (n - 1 - i, 0)),
                   pl.BlockSpec((2, 1, WIDTH), lambda i: (0, 0, 0))],
        out_shape=[jax.ShapeDtypeStruct((n_tok, WIDTH), F32), jax.ShapeDtypeStruct((n_tok, WIDTH), F32),
                   jax.ShapeDtypeStruct((2, 1, WIDTH), F32)],
        scratch_shapes=[pltpu.VMEM((2, 1, WIDTH), F32)],
        compiler_params=_params(),
        name="rglru_scan",
    )(xc, xc, w_gate, b_gate, lam, h0)


def _block_rows(x, lane_head):
    return jnp.concatenate([jnp.where(lane_head == h, x, 0.0) for h in range(QUAD // HEAD)], axis=0)


def _interleave(gens):
    results = [None] * len(gens)
    active = list(range(len(gens)))
    while active:
        still = []
        for g in active:
            try:
                next(gens[g])
                still.append(g)
            except StopIteration as stop:
                results[g] = stop.value
        active = still
    return results


def _wkv_kernel(zf_ref, zr_ref, kf_ref, kr_ref, w2_ref, a2_ref, w0_ref, a0_ref, ka_ref, rk_ref, bd_ref, s0_ref,
                yf_ref, yr_ref, bf_ref, br_ref, sfin_ref, s_scr, *, n_chunk):
    i = pl.program_id(0)
    t = WKV_CHUNK

    @pl.when(i == 0)
    def _():
        s_scr[...] = s0_ref[...]

    row = lax.broadcasted_iota(jnp.int32, (t, QUAD), 0)
    lane = lax.broadcasted_iota(jnp.int32, (t, QUAD), 1)
    lane_head = lane >> 6
    lane_tok = lane & (HEAD - 1)
    row_t = lax.broadcasted_iota(jnp.int32, (t, t), 0)
    col_t = lax.broadcasted_iota(jnp.int32, (t, t), 1)
    same_head = (lax.broadcasted_iota(jnp.int32, (QUAD, QUAD), 0) >> 6) == (
        lax.broadcasted_iota(jnp.int32, (QUAD, QUAD), 1) >> 6)
    eye = jnp.where(lane_tok == row, 1.0, 0.0)
    pair_mask = [((row ^ lane_tok) >> lvl) == 1 for lvl in range(int(math.log2(t)))]

    n_quad = WIDTH // QUAD
    bd = functools.partial(_block_rows, lane_head=lane_head)

    def chunk_setup(d, at, rt, bt, kt, vq):
        strict, incl = (lane_tok < row, lane_tok <= row) if d == 0 else (lane_tok > row, lane_tok >= row)
        ar = jnp.concatenate([at, rt], axis=0)
        gb = _mm_nt(ar, bd(bt))
        yield
        gk = _mm_nt(ar, bd(kt))
        yield
        a_ab = jnp.where(strict, gb[0:t], 0.0)
        a_rb = jnp.where(incl, gb[t:2 * t], 0.0)
        a_ak = jnp.where(strict, gk[0:t], 0.0)
        a_rk = jnp.where(incl, gk[t:2 * t], 0.0)
        inv = eye + jnp.where(pair_mask[0], a_ab, 0.0)
        for lvl in range(1, len(pair_mask)):
            half = _mm(inv, bd(jnp.where(pair_mask[lvl], a_ab, 0.0)))
            yield
            inv = inv + _mm(half, bd(inv))
            yield
        v_bd = bd(vq)
        x0 = _mm(a_ak, v_bd)
        yield
        y0 = _mm(a_rk, v_bd)
        yield
        wu = _mm(inv, jnp.concatenate([bd(at), bd(x0)], axis=1))
        yield
        return wu[:, 0:QUAD], wu[:, QUAD:2 * QUAD], a_rb, y0

    def state_chain(d, q, order, pre, post, y_ref):
        s = s_scr[d, q]
        for c in order:
            w1, u0, a_rb, y0 = pre[(d, c, q)]
            rt, vq, b_end, k_end, decay = post[(d, c, q)]
            u = _mm_nt(w1, s) + u0
            yield
            y = _mm_nt(rt, s) + y0
            yield
            y = y + _mm(a_rb, bd(u))
            yield
            upd = _mm_tn(jnp.concatenate([u, vq], axis=0), jnp.concatenate([b_end, k_end], axis=0))
            yield
            s = s * decay + jnp.where(same_head, upd, 0.0)
            y_ref[c * t:(c + 1) * t, q * QUAD:(q + 1) * QUAD] = y
        s_scr[d, q] = s

    setups, post = {}, {}
    dirs = ((zf_ref, kf_ref, yf_ref, bf_ref), (zr_ref, kr_ref, yr_ref, br_ref))
    for d, (z_ref, k_ref, y_ref, b_ref) in enumerate(dirs):
        r = z_ref[:, 0:WIDTH]
        k = z_ref[:, WIDTH:2 * WIDTH]
        v = z_ref[:, 2 * WIDTH:3 * WIDTH]
        wd = z_ref[:, 3 * WIDTH:3 * WIDTH + LANE_TILE]
        ad = z_ref[:, 3 * WIDTH + LANE_TILE:3 * WIDTH + 2 * LANE_TILE]
        kkn = k_ref[...]
        w_log = -_softplus(-(w0_ref[d] + _mm(jnp.tanh(wd), w2_ref[d]))) - 0.5
        lw = -jnp.exp(w_log)
        iclr = jax.nn.sigmoid(a0_ref[d] + _mm(ad, a2_ref[d]))
        kd = k * (1.0 + (iclr - 1.0) * ka_ref[...])
        kb = kkn * iclr
        b_ref[...] = _head_sum(r * kd * rk_ref[...], bd_ref[...]) * v
        p1 = _bf(lw)
        p2 = _bf(lw - p1.astype(F32))
        lw2 = jnp.concatenate([p1, p2], axis=1)
        tri = jnp.where(col_t <= row_t if d == 0 else col_t >= row_t, 1.0, 0.0).astype(BF16)
        for c in range(n_chunk):
            rows_c = slice(c * t, (c + 1) * t)
            cum2 = jnp.dot(tri, lw2[rows_c], preferred_element_type=F32)
            cum = cum2[:, 0:WIDTH] + cum2[:, WIDTH:2 * WIDTH]
            tot = cum[t - 1:t, :] if d == 0 else cum[0:1, :]
            e_neg = jnp.exp(-cum)
            e_end = jnp.exp(tot - cum)
            a_t = -kkn[rows_c] * jnp.exp(cum - lw[rows_c])
            r_t = r[rows_c] * jnp.exp(cum)
            b_t = kb[rows_c] * e_neg
            k_t = kd[rows_c] * e_neg
            b_end = kb[rows_c] * e_end
            k_end = kd[rows_c] * e_end
            decay_tot = jnp.exp(tot)
            for q in range(n_quad):
                sl = slice(q * QUAD, (q + 1) * QUAD)
                vq = v[rows_c, sl]
                setups[(d, c, q)] = chunk_setup(d, a_t[:, sl], r_t[:, sl], b_t[:, sl], k_t[:, sl], vq)
                post[(d, c, q)] = (r_t[:, sl], vq, b_end[:, sl], k_end[:, sl], decay_tot[:, sl])

    keys = list(setups)
    pre = dict(zip(keys, _interleave([setups[key] for key in keys])))
    chains = []
    for d, y_ref in ((0, yf_ref), (1, yr_ref)):
        order = range(n_chunk) if d == 0 else range(n_chunk - 1, -1, -1)
        chains += [state_chain(d, q, order, pre, post, y_ref) for q in range(n_quad)]
    _interleave(chains)
    sfin_ref[...] = s_scr[...]


def _wkv(zbs, kkn, layer, w2p, a2p, w0, a0, ka, rk, bd_ones, s0):
    n_tok = zbs.shape[0]
    t = min(256, n_tok)
    n = n_tok // t
    fwd = lambda i: (i, 0)
    rev = lambda i: (n - 1 - i, 0)
    state_shape = s0.shape
    return pl.pallas_call(
        functools.partial(_wkv_kernel, n_chunk=t // WKV_CHUNK),
        grid=(n,),
        in_specs=[pl.BlockSpec((t, RWKV_IN), fwd), pl.BlockSpec((t, RWKV_IN), rev),
                  pl.BlockSpec((t, WIDTH), fwd), pl.BlockSpec((t, WIDTH), rev),
                  _layer_spec(w2p, layer), _layer_spec(a2p, layer), _layer_spec(w0, layer),
                  _layer_spec(a0, layer), _layer_spec(ka, layer), _layer_spec(rk, layer),
                  _full_spec(bd_ones), _full_spec(s0)],
        out_specs=[pl.BlockSpec((t, WIDTH), fwd), pl.BlockSpec((t, WIDTH), rev),
                   pl.BlockSpec((t, WIDTH), fwd), pl.BlockSpec((t, WIDTH), rev),
                   pl.BlockSpec(state_shape, lambda i: (0, 0, 0, 0))],
        out_shape=[jax.ShapeDtypeStruct((n_tok, WIDTH), F32)] * 4 + [jax.ShapeDtypeStruct(state_shape, F32)],
        scratch_shapes=[pltpu.VMEM(state_shape, F32)],
        compiler_params=_params(),
        name="wkv7_scan",
    )(zbs, zbs, kkn, kkn, w2p, a2p, w0, a0, ka, rk, bd_ones, s0)


def _cmul_rows(x, p):
    half = LANE_TILE // 2
    lane = lax.broadcasted_iota(jnp.int32, p.shape, 1)
    p_sw = pltpu.roll(p, half, 1)
    p1 = jnp.where(lane < half, p, p_sw)[0:1, :]
    p2 = jnp.where(lane < half, -p_sw, p)[0:1, :]
    return x * p1 + pltpu.roll(x, half, 1) * p2


def _s5_kernel(u_ref, w_ref, v_ref, lam_ref, h0_ref, y_ref, fin_ref, *, nc):
    blk = S5_CHUNK * S5_GROUP
    res = _mm(u_ref[...], w_ref[...])
    row = lax.broadcasted_iota(jnp.int32, (nc, LANE_TILE), 0)
    h_in = []
    for d in range(2):
        e = res[:, blk + d * LANE_TILE:blk + (d + 1) * LANE_TILE]
        lam = jnp.broadcast_to(lam_ref[d], (SUBLANE_TILE, LANE_TILE))
        h0 = jnp.broadcast_to(h0_ref[d], (SUBLANE_TILE, LANE_TILE))
        first = 0 if d == 0 else nc - 1
        e = jnp.where(row == first, e + _cmul_rows(h0, lam)[0:1, :], e)
        p = lam
        s = 1
        while s < nc:
            if d == 0:
                sh, m = pltpu.roll(e, s, 0), row >= s
            else:
                sh, m = pltpu.roll(e, nc - s, 0), row < nc - s
            e = e + jnp.where(m, _cmul_rows(sh, p), 0.0)
            p = _cmul_rows(p, p)
            s *= 2
        if d == 0:
            fin_ref[d] = e[nc - 1:nc, :]
            h_in.append(jnp.where(row >= 1, pltpu.roll(e, 1, 0), h0[0:1, :]))
        else:
            fin_ref[d] = e[0:1, :]
            h_in.append(jnp.where(row < nc - 1, pltpu.roll(e, nc - 1, 0), h0[0:1, :]))
    y_ref[...] = res[:, 0:blk] + _mm(jnp.concatenate(h_in, axis=1), v_ref[...])


def _s5(u_blocks, layer, w_cat, v_cat, lam16, h0):
    groups, nc, blk = u_blocks.shape
    kern = functools.partial(_s5_kernel, nc=nc)
    return pl.pallas_call(
        kern,
        grid=(groups,),
        in_specs=[pl.BlockSpec((None, nc, blk), lambda g: (g, 0, 0)),
                  pl.BlockSpec((None, None, blk, w_cat.shape[-1]), lambda g: (layer, g, 0, 0)),
                  pl.BlockSpec((None, None, blk, blk), lambda g: (layer, g, 0, 0)),
                  pl.BlockSpec((None, 2, None, 1, LANE_TILE), lambda g: (layer, 0, g, 0, 0)),
                  pl.BlockSpec((2, None, 1, LANE_TILE), lambda g: (0, g, 0, 0))],
        out_specs=[pl.BlockSpec((None, nc, blk), lambda g: (g, 0, 0)),
                   pl.BlockSpec((2, None, 1, LANE_TILE), lambda g: (0, g, 0, 0))],
        out_shape=[jax.ShapeDtypeStruct((groups, nc, blk), F32),
                   jax.ShapeDtypeStruct((2, groups, 1, LANE_TILE), F32)],
        compiler_params=_params(),
        name="s5_scan",
    )(u_blocks, w_cat, v_cat, lam16, h0)


def _s5_weights(lam_re, lam_im, log_step, b_re, b_im, c_re, c_im):
    n = S5_CHUNK
    step = jnp.exp(log_step)[..., None]
    x_re, ang = lam_re * step, lam_im * step
    mag = jnp.exp(x_re)
    lb_re, lb_im = mag * jnp.cos(ang), mag * jnp.sin(ang)
    nr = jnp.expm1(x_re) * jnp.cos(ang) - 2.0 * jnp.square(jnp.sin(0.5 * ang))
    den = lam_re * lam_re + lam_im * lam_im
    f_re = (nr * lam_re + lb_im * lam_im) / den
    f_im = (lb_im * lam_re - nr * lam_im) / den
    bb_re = f_re[..., None] * b_re - f_im[..., None] * b_im
    bb_im = f_re[..., None] * b_im + f_im[..., None] * b_re
    pr, pi = [jnp.ones_like(lb_re)], [jnp.zeros_like(lb_re)]
    for _ in range(n):
        pr, pi = pr + [pr[-1] * lb_re - pi[-1] * lb_im], pi + [pr[-1] * lb_im + pi[-1] * lb_re]
    pw_re, pw_im = jnp.stack(pr, axis=-2), jnp.stack(pi, axis=-2)
    cp_re = c_re[..., None, :, :] * pw_re[..., :, None, :] - c_im[..., None, :, :] * pw_im[..., :, None, :]
    cp_im = c_re[..., None, :, :] * pw_im[..., :, None, :] + c_im[..., None, :, :] * pw_re[..., :, None, :]
    taps = jnp.sum(cp_re[..., :, None] * bb_re[..., None, None, :, :]
                   - cp_im[..., :, None] * bb_im[..., None, None, :, :], axis=-2)
    fwd, bwd = taps[:, 0], taps[:, 1]
    both = jnp.concatenate([jnp.flip(bwd[..., 1:n, :, :], axis=-3), fwd[..., 0:1, :, :] + bwd[..., 0:1, :, :],
                            fwd[..., 1:n, :, :]], axis=-3)
    lag = jnp.arange(n)[None, :] - jnp.arange(n)[:, None] + (n - 1)
    m_tot = jnp.take(_bf(both), lag.reshape(-1), axis=-3)
    m_tot = m_tot.reshape(both.shape[:-3] + (n, n, S5_GROUP, S5_GROUP))
    m_tot = jnp.swapaxes(jnp.swapaxes(m_tot, -1, -2), -2, -3)
    m_tot = m_tot.reshape(both.shape[:-3] + (n * S5_GROUP, n * S5_GROUP))

    def end_state(d, order):
        qr, qi = pw_re[:, d][..., order, :], pw_im[:, d][..., order, :]
        er = qr[..., :, None, :] * jnp.swapaxes(bb_re[:, d], -1, -2)[..., None, :, :] \
            - qi[..., :, None, :] * jnp.swapaxes(bb_im[:, d], -1, -2)[..., None, :, :]
        ei = qr[..., :, None, :] * jnp.swapaxes(bb_im[:, d], -1, -2)[..., None, :, :] \
            + qi[..., :, None, :] * jnp.swapaxes(bb_re[:, d], -1, -2)[..., None, :, :]
        e = jnp.concatenate([er, ei], axis=-1)
        return e.reshape(e.shape[:-3] + (n * S5_GROUP, 2 * S5_STATE))

    def state_out(d, order):
        vr = jnp.moveaxis(cp_re[:, d][..., order, :, :], -1, -3)
        vi = -jnp.moveaxis(cp_im[:, d][..., order, :, :], -1, -3)
        vv = jnp.concatenate([vr, vi], axis=-3)
        return vv.reshape(vv.shape[:-3] + (2 * S5_STATE, n * S5_GROUP))

    asc = jnp.arange(n)
    w_cat = jnp.concatenate([m_tot, _bf(end_state(0, n - 1 - asc)), _bf(end_state(1, asc))], axis=-1)
    v_cat = jnp.concatenate([state_out(0, asc + 1), state_out(1, n - asc)], axis=-2)
    lam16 = jnp.concatenate([pw_re[..., n, :], pw_im[..., n, :]], axis=-1)[..., None, :]
    return _bf(w_cat), _bf(v_cat), lam16


def _merge_kernel(h_ref, hf_ref, hr_ref, ga_ref, yf_ref, yr_ref, bf_ref, br_ref, gd_ref, ys_ref, uc_ref,
                  mod_ref, g_ref, wzg_ref, lnw_ref, lnb_ref, g2_ref, dsk_ref, wglu_ref, bglu_ref,
                  wbr_ref, wout_ref, bd_ref, o_ref, *, row):
    h = h_ref[...]
    shift = mod_ref[row:row + 1, 0:D]
    scale = mod_ref[row:row + 1, D:2 * D]
    gate = mod_ref[row:row + 1, 2 * D:3 * D]
    zg = _mm(_mod_norm(h, g_ref[...], shift, scale), wzg_ref[...])
    bd = bd_ref[...]
    y_a = jax.nn.gelu(ga_ref[...]) * (hf_ref[...] + hr_ref[...])
    y = yf_ref[...] + yr_ref[...]
    inv_n = 1.0 / HEAD
    yc = y - _head_sum(y, bd) * inv_n
    var = _head_sum(yc * yc, bd) * inv_n
    y_b = yc * lax.rsqrt(var + GN_EPS) * lnw_ref[...] + lnb_ref[...] + (bf_ref[...] + br_ref[...])
    y_b = y_b * _mm(jax.nn.sigmoid(gd_ref[...]), g2_ref[...])
    y_s = jax.nn.gelu(ys_ref[...] + dsk_ref[...] * uc_ref[...])
    y_c = y_s * jax.nn.sigmoid(_mm(y_s, wglu_ref[...]) + bglu_ref[...])
    mix = (jax.nn.sigmoid(zg[:, 0:D]) * _mm(y_a, wbr_ref[0])
           + jax.nn.sigmoid(zg[:, D:2 * D]) * _mm(y_b, wbr_ref[1])
           + jax.nn.sigmoid(zg[:, 2 * D:3 * D]) * _mm(y_c, wbr_ref[2]))
    o_ref[...] = h + gate * _mm(mix, wout_ref[...])


def _merge(h, parts, layer, row, mod, norm1, w_zg, lnw, lnb, g2, dsk, w_glu, b_glu, w_branch, w_out, bd_ones):
    n_tok = h.shape[0]
    tm = min(256, n_tok)
    hf, hr, ga, yf, yr, bf_, br_, zbs, ys5, uc = parts
    kern = functools.partial(_merge_kernel, row=row)
    rs = _row_spec(tm, WIDTH)
    gd_col = (3 * WIDTH + 2 * LANE_TILE) // LANE_TILE
    weights = (mod, norm1, w_zg, lnw, lnb, g2, dsk, w_glu, b_glu, w_branch, w_out)
    return pl.pallas_call(
        kern,
        grid=(n_tok // tm,),
        in_specs=[_row_spec(tm, D), rs, rs, rs, rs, rs, rs, rs, _row_spec(tm, LANE_TILE, gd_col), rs, rs]
        + [_layer_spec(w, layer) for w in weights] + [_full_spec(bd_ones)],
        out_specs=_row_spec(tm, D),
        out_shape=jax.ShapeDtypeStruct((n_tok, D), F32),
        compiler_params=_params(),
        name="merge_project",
    )(h, hf, hr, ga, yf, yr, bf_, br_, zbs, ys5, uc, *weights, bd_ones)


def _ffn_kernel(h_ref, mod_ref, g_ref, win_ref, wout_ref, o_ref, *, row):
    h = h_ref[...]
    shift = mod_ref[row:row + 1, 3 * D:4 * D]
    scale = mod_ref[row:row + 1, 4 * D:5 * D]
    gate = mod_ref[row:row + 1, 5 * D:6 * D]
    gu = _mm(_mod_norm(h, g_ref[...], shift, scale), win_ref[...])
    a, up = gu[:, 0:FFN_HIDDEN], gu[:, FFN_HIDDEN:2 * FFN_HIDDEN]
    o_ref[...] = h + gate * _mm(a * jax.nn.sigmoid(a) * up, wout_ref[...])


def _ffn(h, layer, row, mod, norm2, w_in, w_out):
    n_tok = h.shape[0]
    tm = min(256, n_tok)
    kern = functools.partial(_ffn_kernel, row=row)
    return pl.pallas_call(
        kern,
        grid=(n_tok // tm,),
        in_specs=[_row_spec(tm, D), _layer_spec(mod, layer), _layer_spec(norm2, layer),
                  _layer_spec(w_in, layer), _layer_spec(w_out, layer)],
        out_specs=_row_spec(tm, D),
        out_shape=jax.ShapeDtypeStruct((n_tok, D), F32),
        compiler_params=_params(),
        name="swiglu",
    )(h, mod, norm2, w_in, w_out)


def _final_norm_kernel(h_ref, g_ref, o_ref):
    x = h_ref[...]
    o_ref[...] = x * lax.rsqrt(jnp.mean(x * x, axis=-1, keepdims=True) + EPS) * g_ref[...]


def _final_norm(h, g):
    n_tok = h.shape[0]
    tm = min(512, n_tok)
    return pl.pallas_call(
        _final_norm_kernel,
        grid=(n_tok // tm,),
        in_specs=[_row_spec(tm, D), _full_spec(g)],
        out_specs=_row_spec(tm, D),
        out_shape=jax.ShapeDtypeStruct((n_tok, D), F32),
        compiler_params=_params(),
        name="final_norm",
    )(h, g)


def _block_diag(w):
    n_h, a, b = w.shape[-3:]
    out = w[..., :, :, None, :] * jnp.eye(n_h, dtype=w.dtype)[:, None, :, None]
    return out.reshape(w.shape[:-3] + (n_h * a, n_h * b))


def _pad_lora(w):
    z = jnp.zeros_like(w[:, 0])
    return jnp.stack([jnp.concatenate([w[:, 0], z], axis=1), jnp.concatenate([z, w[:, 1]], axis=1)], axis=1)


def _to_s5_blocks(u, col_major):
    n_tok = u.shape[0]
    if col_major:
        rows = n_tok // GRID_W
        x = u.reshape(rows // S5_CHUNK, S5_CHUNK, GRID_W, S5_GROUPS, S5_GROUP)
        x = x.transpose(3, 2, 0, 1, 4)
    else:
        x = u.reshape(n_tok // S5_CHUNK, S5_CHUNK, S5_GROUPS, S5_GROUP).transpose(2, 0, 1, 3)
    return x.reshape(S5_GROUPS, n_tok // S5_CHUNK, S5_CHUNK * S5_GROUP)


def _from_s5_blocks(y, n_tok, col_major):
    if col_major:
        rows = n_tok // GRID_W
        x = y.reshape(S5_GROUPS, GRID_W, rows // S5_CHUNK, S5_CHUNK, S5_GROUP).transpose(2, 3, 1, 0, 4)
    else:
        x = y.reshape(S5_GROUPS, n_tok // S5_CHUNK, S5_CHUNK, S5_GROUP).transpose(1, 2, 0, 3)
    return x.reshape(n_tok, WIDTH)


def _prepare(c, c_ctx, w_mod, b_mod, norm1, norm2, w_in, lru_conv_w, lru_conv_b, lru_wa, lru_ba, lru_wx, lru_bx,
             lru_lam, rwkv_mu, rwkv_w0, rwkv_w2, rwkv_a0, rwkv_a2, rwkv_g2, rwkv_kk, rwkv_ka, rwkv_rk, rwkv_lnw,
             rwkv_lnb, s5_lam_re, s5_lam_im, s5_log_step, s5_b_re, s5_b_im, s5_c_re, s5_c_im, s5_d, s5_w_glu,
             s5_b_glu, w_branch, w_out, w_ffn_in, w_ffn_out):
    depth = w_in.shape[0]
    vec = lambda a: a.reshape(depth, 1, a.shape[-1])
    cond = jnp.concatenate([c, c_ctx[None], jnp.zeros((SUBLANE_TILE - 2, D), F32)], axis=0)
    w_a, w_zg = _cast_bf16(w_in, ((0, N_A), (N_A, w_in.shape[-1])))
    s5_w, s5_v, s5_lam16 = _s5_weights(s5_lam_re, s5_lam_im, s5_log_step, s5_b_re, s5_b_im, s5_c_re, s5_c_im)
    head_id = jnp.arange(WIDTH) // HEAD
    return dict(
        mod=_modulation(cond, w_mod, b_mod),
        w_a=w_a, w_zg=w_zg,
        w_gate=_bf(jnp.concatenate([_block_diag(lru_wa), _block_diag(lru_wx)], axis=-1)),
        b_gate=jnp.concatenate([lru_ba, lru_bx], axis=-1)[:, :, None, :],
        lam=lru_lam[:, :, None, :],
        w2p=_bf(_pad_lora(rwkv_w2)), a2p=_bf(_pad_lora(rwkv_a2)),
        w0=rwkv_w0[:, :, None, :], a0=rwkv_a0[:, :, None, :], rk=rwkv_rk.reshape(depth, 1, WIDTH),
        s5_w=s5_w, s5_v=s5_v, s5_lam16=s5_lam16,
        w_branch=_cast_bf16(w_branch)[0], w_out=_cast_bf16(w_out)[0],
        w_ffn_in=_cast_bf16(w_ffn_in)[0], w_ffn_out=_cast_bf16(w_ffn_out)[0],
        g2=_bf(rwkv_g2), w_glu=_bf(s5_w_glu), n1=vec(norm1), n2=vec(norm2), conv_w=lru_conv_w,
        conv_b=vec(lru_conv_b), mu=vec(rwkv_mu), kk_w=vec(rwkv_kk), ka=vec(rwkv_ka), lnw=vec(rwkv_lnw),
        lnb=vec(rwkv_lnb), dsk=vec(s5_d), b_glu=vec(s5_b_glu),
        bd_ones=(head_id[:, None] == head_id[None, :]).astype(BF16),
    )


def _zero_states():
    return (jnp.zeros((2, 1, WIDTH), F32),
            jnp.zeros((2, WIDTH // QUAD, QUAD, QUAD), F32),
            jnp.zeros((2, S5_GROUPS, 1, LANE_TILE), F32))


def _mixers(p, h, layer, row, states, col_major):
    n_tok = h.shape[0]
    xc, ga, zbs, uc, kkn = _premix(h, layer, row, p["mod"], p["n1"], p["w_a"], p["conv_w"], p["conv_b"],
                                   p["mu"], p["kk_w"], p["bd_ones"])
    hf, hr, fin_lru = _lru(xc, layer, p["w_gate"], p["b_gate"], p["lam"], states[0])
    yf, yr, bf_, br_, fin_wkv = _wkv(zbs, kkn, layer, p["w2p"], p["a2p"], p["w0"], p["a0"], p["ka"], p["rk"],
                                     p["bd_ones"], states[1])
    ys5, fin_s5 = _s5(_to_s5_blocks(uc, col_major), layer, p["s5_w"], p["s5_v"], p["s5_lam16"], states[2])
    ys5 = _from_s5_blocks(ys5, n_tok, col_major)
    return (hf, hr, ga, yf, yr, bf_, br_, zbs, ys5, uc), (fin_lru, fin_wkv, fin_s5)


def _channel_mix(p, h, parts, layer, row):
    h = _merge(h, parts, layer, row, p["mod"], p["n1"], p["w_zg"], p["lnw"], p["lnb"], p["g2"], p["dsk"],
               p["w_glu"], p["b_glu"], p["w_branch"], p["w_out"], p["bd_ones"])
    return _ffn(h, layer, row, p["mod"], p["n2"], p["w_ffn_in"], p["w_ffn_out"])


def kernel(x, c, ctx, c_ctx, w_mod, b_mod, norm1, norm2, norm_f, w_in, lru_conv_w, lru_conv_b, lru_wa, lru_ba,
           lru_wx, lru_bx, lru_lam, rwkv_mu, rwkv_w0, rwkv_w2, rwkv_a0, rwkv_a2, rwkv_g2, rwkv_kk, rwkv_ka,
           rwkv_rk, rwkv_lnw, rwkv_lnb, s5_lam_re, s5_lam_im, s5_log_step, s5_b_re, s5_b_im, s5_c_re, s5_c_im,
           s5_d, s5_w_glu, s5_b_glu, w_branch, w_out, w_ffn_in, w_ffn_out):
    bsz, n_lat, d_model = x.shape
    n_ctx = ctx.shape[1]
    depth = w_in.shape[0]
    assert bsz == 1 and d_model == D
    assert n_lat % (GRID_W * S5_CHUNK) == 0 and n_lat % 256 == 0
    assert n_ctx % WKV_CHUNK == 0 and (n_ctx <= 256 or n_ctx % 256 == 0)
    p = _prepare(c, c_ctx, w_mod, b_mod, norm1, norm2, w_in, lru_conv_w, lru_conv_b, lru_wa, lru_ba, lru_wx,
                 lru_bx, lru_lam, rwkv_mu, rwkv_w0, rwkv_w2, rwkv_a0, rwkv_a2, rwkv_g2, rwkv_kk, rwkv_ka, rwkv_rk,
                 rwkv_lnw, rwkv_lnb, s5_lam_re, s5_lam_im, s5_log_step, s5_b_re, s5_b_im, s5_c_re, s5_c_im, s5_d,
                 s5_w_glu, s5_b_glu, w_branch, w_out, w_ffn_in, w_ffn_out)
    h_lat, h_ctx = x[0], ctx[0]
    for layer in range(depth):
        parts_c, states = _mixers(p, h_ctx, layer, 1, _zero_states(), False)
        parts_l, _ = _mixers(p, h_lat, layer, 0, states, True)
        h_lat = _channel_mix(p, h_lat, parts_l, layer, 0)
        if layer != depth - 1:
            h_ctx = _channel_mix(p, h_ctx, parts_c, layer, 1)
    return _final_norm(h_lat, norm_f.reshape(1, D))[None]
```

```python
import functools
import math

import jax
import jax.numpy as jnp
from jax import lax
from jax.experimental import pallas as pl
from jax.experimental.pallas import tpu as pltpu

F32 = jnp.float32
BF16 = jnp.bfloat16
HIGHEST = lax.Precision.HIGHEST

D = 1024
WIDTH = 512
GRID_W = 64
EPS = 1e-6
LRU_C = 8.0
HEAD = 64
RWKV_IN = 3 * WIDTH + 2 * 64 + 2 * 64 + 128
GN_EPS = 64e-5
S5_GROUPS = 32
S5_GROUP = 16
S5_STATE = 64
S5_CHUNK = 16
N_A = 2 * WIDTH + RWKV_IN + WIDTH
FFN_HIDDEN = 2816
WKV_CHUNK = 64
LANE_TILE = 128
SUBLANE_TILE = 8
QUAD = 4 * HEAD
VMEM_LIMIT = 56 * 1024 * 1024


def _bf(x):
    return x.astype(BF16)


def _mm(a, b):
    return jnp.dot(_bf(a), _bf(b), preferred_element_type=F32)


def _mm_nt(a, b):
    return lax.dot_general(_bf(a), _bf(b), (((1,), (1,)), ((), ())), preferred_element_type=F32)


def _mm_tn(a, b):
    return lax.dot_general(_bf(a), _bf(b), (((0,), (0,)), ((), ())), preferred_element_type=F32)


def _softplus(x):
    return jnp.maximum(x, 0.0) + jnp.log(1.0 + jnp.exp(-jnp.abs(x)))


def _head_sum(x, bd_ones):
    hi = _bf(x)
    lo = _bf(x - hi.astype(F32))
    return (jnp.dot(hi, bd_ones, preferred_element_type=F32)
            + jnp.dot(lo, bd_ones, preferred_element_type=F32))


def _mod_norm(x, g, shift, scale):
    y = x * lax.rsqrt(jnp.mean(x * x, axis=-1, keepdims=True) + EPS) * g
    return y * (1.0 + scale) + shift


def _params(n_axes=1):
    return pltpu.CompilerParams(dimension_semantics=("arbitrary",) * n_axes,
                                vmem_limit_bytes=VMEM_LIMIT)


def _layer_spec(arr, layer):
    rest = arr.shape[1:]
    return pl.BlockSpec((None,) + rest, lambda *_: (layer,) + (0,) * len(rest))


def _full_spec(arr):
    return pl.BlockSpec(arr.shape, lambda *_: (0,) * arr.ndim)


def _row_spec(tm, width, col=0):
    return pl.BlockSpec((tm, width), lambda i: (i, col))


def _cast_kernel(x_ref, *o_refs, splits):
    for o_ref, (lo, hi) in zip(o_refs, splits):
        o_ref[...] = _bf(x_ref[:, lo:hi])


def _cast_bf16(w, splits=None):
    lead, (rows, cols) = w.shape[:-2], w.shape[-2:]
    splits = splits or ((0, cols),)
    n_rows = math.prod(lead) * rows
    tm = 256
    outs = pl.pallas_call(
        functools.partial(_cast_kernel, splits=splits),
        grid=(n_rows // tm,),
        in_specs=[_row_spec(tm, cols)],
        out_specs=[_row_spec(tm, hi - lo) for lo, hi in splits],
        out_shape=[jax.ShapeDtypeStruct((n_rows, hi - lo), BF16) for lo, hi in splits],
        compiler_params=_params(),
        name="cast_bf16",
    )(w.reshape(n_rows, cols))
    return [o.reshape(lead + (rows, hi - lo)) for o, (lo, hi) in zip(outs, splits)]


def _mod_kernel(cond_ref, w_ref, b_ref, o_ref):
    cnd = cond_ref[...]
    act = cnd * jax.nn.sigmoid(cnd)
    o_ref[...] = jnp.dot(act, w_ref[...], preferred_element_type=F32, precision=HIGHEST) + b_ref[...]


def _modulation(cond, w_mod, b_mod):
    depth = w_mod.shape[0]
    n_col = w_mod.shape[2] // D
    return pl.pallas_call(
        _mod_kernel,
        grid=(depth, n_col),
        in_specs=[pl.BlockSpec((SUBLANE_TILE, D), lambda l, j: (0, 0)),
                  pl.BlockSpec((None, D, D), lambda l, j: (l, 0, j)),
                  pl.BlockSpec((None, 1, D), lambda l, j: (l, 0, j))],
        out_specs=pl.BlockSpec((None, SUBLANE_TILE, D), lambda l, j: (l, 0, j)),
        out_shape=jax.ShapeDtypeStruct((depth, SUBLANE_TILE, 6 * D), F32),
        compiler_params=_params(2),
        name="modulation",
    )(cond, w_mod, b_mod.reshape(depth, 1, 6 * D))


def _premix_kernel(h_ref, hp_ref, hn_ref, mod_ref, g_ref, w_ref, cw_ref, cb_ref, mu_ref, kk_ref, bd_ref,
                   xc_ref, ga_ref, zb_ref, uc_ref, kkn_ref, z_scr, *, row, tm):
    i = pl.program_id(0)
    n = pl.num_programs(0)
    halo = SUBLANE_TILE
    shift = mod_ref[row:row + 1, 0:D]
    scale = mod_ref[row:row + 1, D:2 * D]
    hext = jnp.concatenate([hp_ref[...], h_ref[...], hn_ref[...]], axis=0)
    xn = _mod_norm(hext, g_ref[...], shift, scale)
    z_scr[...] = _mm(xn, w_ref[...])

    @pl.when(i == 0)
    def _():
        z_scr[0:halo, :] = jnp.zeros((halo, N_A), F32)

    @pl.when(i == n - 1)
    def _():
        z_scr[tm + halo:tm + 2 * halo, :] = jnp.zeros((halo, N_A), F32)

    acc = cb_ref[...] + cw_ref[0:1, :] * z_scr[halo - 2:halo - 2 + tm, 0:WIDTH]
    for j in range(1, 4):
        acc = acc + cw_ref[j:j + 1, :] * z_scr[halo - 2 + j:halo - 2 + j + tm, 0:WIDTH]
    xc_ref[...] = acc
    ga_ref[...] = z_scr[halo:halo + tm, WIDTH:2 * WIDTH]
    lo, hi = 2 * WIDTH, 2 * WIDTH + RWKV_IN
    zc = z_scr[halo:halo + tm, lo:hi]
    zp = z_scr[halo - 1:halo - 1 + tm, lo:hi]
    zn = z_scr[halo + 1:halo + 1 + tm, lo:hi]
    zs = zc + mu_ref[...] * (0.5 * (zp + zn) - zc)
    zb_ref[...] = zs
    uc_ref[...] = z_scr[halo:halo + tm, hi:hi + WIDTH]
    kk = zs[:, WIDTH:2 * WIDTH] * kk_ref[...]
    ss = _head_sum(kk * kk, bd_ref[...])
    kkn_ref[...] = kk * lax.rsqrt(ss + 1e-12)


def _premix(h, layer, row, mod, norm1, w_a, conv_w, conv_b, mu, kk_w, bd_ones):
    n_tok = h.shape[0]
    tm = min(256, n_tok)
    n = n_tok // tm
    per = tm // SUBLANE_TILE
    last_blk = n_tok // SUBLANE_TILE - 1
    kern = functools.partial(_premix_kernel, row=row, tm=tm)
    outs = pl.pallas_call(
        kern,
        grid=(n,),
        in_specs=[
            _row_spec(tm, D),
            pl.BlockSpec((SUBLANE_TILE, D), lambda i: (jnp.maximum(i * per - 1, 0), 0)),
            pl.BlockSpec((SUBLANE_TILE, D), lambda i: (jnp.minimum((i + 1) * per, last_blk), 0)),
            _layer_spec(mod, layer), _layer_spec(norm1, layer), _layer_spec(w_a, layer),
            _layer_spec(conv_w, layer), _layer_spec(conv_b, layer), _layer_spec(mu, layer),
            _layer_spec(kk_w, layer), _full_spec(bd_ones),
        ],
        out_specs=[_row_spec(tm, WIDTH), _row_spec(tm, WIDTH), _row_spec(tm, RWKV_IN),
                   _row_spec(tm, WIDTH), _row_spec(tm, WIDTH)],
        out_shape=[jax.ShapeDtypeStruct((n_tok, WIDTH), F32), jax.ShapeDtypeStruct((n_tok, WIDTH), F32),
                   jax.ShapeDtypeStruct((n_tok, RWKV_IN), F32), jax.ShapeDtypeStruct((n_tok, WIDTH), F32),
                   jax.ShapeDtypeStruct((n_tok, WIDTH), F32)],
        scratch_shapes=[pltpu.VMEM((tm + 2 * SUBLANE_TILE, N_A), F32)],
        compiler_params=_params(),
        name="premix",
    )(h, h, h, mod, norm1, w_a, conv_w, conv_b, mu, kk_w, bd_ones)
    return outs


def _lru_kernel(xf_ref, xr_ref, wg_ref, bg_ref, lam_ref, h0_ref, hf_ref, hr_ref, fin_ref, carry, *, tm):
    i = pl.program_id(0)

    @pl.when(i == 0)
    def _():
        carry[...] = h0_ref[...]

    rows = lax.broadcasted_iota(jnp.int32, (tm, WIDTH), 0)
    for d, (x_ref, o_ref) in enumerate(((xf_ref, hf_ref), (xr_ref, hr_ref))):
        xc = x_ref[...]
        gates = _mm(xc, wg_ref[d]) + bg_ref[d]
        gate_r = jax.nn.sigmoid(gates[:, 0:WIDTH])
        gate_i = jax.nn.sigmoid(gates[:, WIDTH:2 * WIDTH])
        log_a = -LRU_C * gate_r * _softplus(-lam_ref[d])
        a = jnp.exp(log_a)
        b = jnp.sqrt(1.0 - jnp.exp(2.0 * log_a)) * (gate_i * xc)
        s = 1
        while s < tm:
            if d == 0:
                a_sh, b_sh, m = pltpu.roll(a, s, 0), pltpu.roll(b, s, 0), rows >= s
            else:
                a_sh, b_sh, m = pltpu.roll(a, tm - s, 0), pltpu.roll(b, tm - s, 0), rows < tm - s
            b = jnp.where(m, a * b_sh + b, b)
            a = jnp.where(m, a * a_sh, a)
            s *= 2
        hs = a * carry[d] + b
        o_ref[...] = hs
        carry[d] = hs[tm - 1:tm, :] if d == 0 else hs[0:1, :]
    fin_ref[...] = carry[...]


def _lru(xc, layer, w_gate, b_gate, lam, h0):
    n_tok = xc.shape[0]
    tm = min(256, n_tok)
    n = n_tok // tm
    kern = functools.partial(_lru_kernel, tm=tm)
    return pl.pallas_call(
        kern,
        grid=(n,),
        in_specs=[_row_spec(tm, WIDTH),
                  pl.BlockSpec((tm, WIDTH), lambda i: (n - 1 - i, 0)),
                  _layer_spec(w_gate, layer), _layer_spec(b_gate, layer), _layer_spec(lam, layer),
                  _full_spec(h0)],
        out_specs=[_row_spec(tm, WIDTH),
                   pl.BlockSpec((tm, WIDTH), lambda i: (n - 1 - i, 0)),
                   pl.BlockSpec((2, 1, WIDTH), lambda i: (0, 0, 0))],
        out_shape=[jax.ShapeDtypeStruct((n_tok, WIDTH), F32), jax.ShapeDtypeStruct((n_tok, WIDTH), F32),
                   jax.ShapeDtypeStruct((2, 1, WIDTH), F32)],
        scratch_shapes=[pltpu.VMEM((2, 1, WIDTH), F32)],
        compiler_params=_params(),
        name="rglru_scan",
    )(xc, xc, w_gate, b_gate, lam, h0)


def _block_rows(x, lane_head):
    return jnp.concatenate([jnp.where(lane_head == h, x, 0.0) for h in range(QUAD // HEAD)], axis=0)


def _interleave(gens):
    results = [None] * len(gens)
    active = list(range(len(gens)))
    while active:
        still = []
        for g in active:
            try:
                next(gens[g])
                still.append(g)
            except StopIteration as stop:
                results[g] = stop.value
        active = still
    return results


def _wkv_kernel(zf_ref, zr_ref, kf_ref, kr_ref, w2_ref, a2_ref, w0_ref, a0_ref, ka_ref, rk_ref, bd_ref, s0_ref,
                yf_ref, yr_ref, bf_ref, br_ref, sfin_ref, s_scr, *, n_chunk):
    i = pl.program_id(0)
    t = WKV_CHUNK

    @pl.when(i == 0)
    def _():
        s_scr[...] = s0_ref[...]

    row = lax.broadcasted_iota(jnp.int32, (t, QUAD), 0)
    lane = lax.broadcasted_iota(jnp.int32, (t, QUAD), 1)
    lane_head = lane >> 6
    lane_tok = lane & (HEAD - 1)
    row_t = lax.broadcasted_iota(jnp.int32, (t, t), 0)
    col_t = lax.broadcasted_iota(jnp.int32, (t, t), 1)
    same_head = (lax.broadcasted_iota(jnp.int32, (QUAD, QUAD), 0) >> 6) == (
        lax.broadcasted_iota(jnp.int32, (QUAD, QUAD), 1) >> 6)
    eye = jnp.where(lane_tok == row, 1.0, 0.0)
    pair_mask = [((row ^ lane_tok) >> lvl) == 1 for lvl in range(int(math.log2(t)))]

    n_quad = WIDTH // QUAD
    bd = functools.partial(_block_rows, lane_head=lane_head)

    def chunk_setup(d, at, rt, bt, kt, vq):
        strict, incl = (lane_tok < row, lane_tok <= row) if d == 0 else (lane_tok > row, lane_tok >= row)
        ar = jnp.concatenate([at, rt], axis=0)
        gb = _mm_nt(ar, bd(bt))
        yield
        gk = _mm_nt(ar, bd(kt))
        yield
        a_ab = jnp.where(strict, gb[0:t], 0.0)
        a_rb = jnp.where(incl, gb[t:2 * t], 0.0)
        a_ak = jnp.where(strict, gk[0:t], 0.0)
        a_rk = jnp.where(incl, gk[t:2 * t], 0.0)
        inv = eye + jnp.where(pair_mask[0], a_ab, 0.0)
        for lvl in range(1, len(pair_mask)):
            half = _mm(inv, bd(jnp.where(pair_mask[lvl], a_ab, 0.0)))
            yield
            inv = inv + _mm(half, bd(inv))
            yield
        v_bd = bd(vq)
        xy0 = _mm(jnp.concatenate([a_ak, a_rk], axis=0), v_bd)
        yield
        x0, y0 = xy0[0:t], xy0[t:2 * t]
        wu = _mm(inv, jnp.concatenate([bd(at), bd(x0)], axis=1))
        yield
        return wu[:, 0:QUAD], wu[:, QUAD:2 * QUAD], a_rb, y0

    def state_chain(d, q, order, pre, post, y_ref):
        s = s_scr[d, q]
        for c in order:
            w1, u0, a_rb, y0 = pre[(d, c, q)]
            rt, vq, b_end, k_end, decay = post[(d, c, q)]
            ws = _mm_nt(jnp.concatenate([w1, rt], axis=0), s)
            yield
            u = ws[0:t] + u0
            y = ws[t:2 * t] + y0 + _mm(a_rb, bd(u))
            yield
            upd = _mm_tn(jnp.concatenate([u, vq], axis=0), jnp.concatenate([b_end, k_end], axis=0))
            yield
            s = s * decay + jnp.where(same_head, upd, 0.0)
            y_ref[c * t:(c + 1) * t, q * QUAD:(q + 1) * QUAD] = y
        s_scr[d, q] = s

    setups, post = {}, {}
    dirs = ((zf_ref, kf_ref, yf_ref, bf_ref), (zr_ref, kr_ref, yr_ref, br_ref))
    for d, (z_ref, k_ref, y_ref, b_ref) in enumerate(dirs):
        r = z_ref[:, 0:WIDTH]
        k = z_ref[:, WIDTH:2 * WIDTH]
        v = z_ref[:, 2 * WIDTH:3 * WIDTH]
        wd = z_ref[:, 3 * WIDTH:3 * WIDTH + LANE_TILE]
        ad = z_ref[:, 3 * WIDTH + LANE_TILE:3 * WIDTH + 2 * LANE_TILE]
        kkn = k_ref[...]
        w_log = -_softplus(-(w0_ref[d] + _mm(jnp.tanh(wd), w2_ref[d]))) - 0.5
        lw = -jnp.exp(w_log)
        iclr = jax.nn.sigmoid(a0_ref[d] + _mm(ad, a2_ref[d]))
        kd = k * (1.0 + (iclr - 1.0) * ka_ref[...])
        kb = kkn * iclr
        b_ref[...] = _mm(r * kd * rk_ref[...], bd_ref[...]) * v
        p1 = _bf(lw)
        p2 = _bf(lw - p1.astype(F32))
        lw2 = jnp.concatenate([p1, p2], axis=1)
        tri = jnp.where(col_t <= row_t if d == 0 else col_t >= row_t, 1.0, 0.0).astype(BF16)
        for c in range(n_chunk):
            rows_c = slice(c * t, (c + 1) * t)
            cum2 = jnp.dot(tri, lw2[rows_c], preferred_element_type=F32)
            cum = cum2[:, 0:WIDTH] + cum2[:, WIDTH:2 * WIDTH]
            tot = cum[t - 1:t, :] if d == 0 else cum[0:1, :]
            e_neg = jnp.exp(-cum)
            e_end = jnp.exp(tot - cum)
            a_t = -kkn[rows_c] * jnp.exp(cum - lw[rows_c])
            r_t = r[rows_c] * jnp.exp(cum)
            b_t = kb[rows_c] * e_neg
            k_t = kd[rows_c] * e_neg
            b_end = kb[rows_c] * e_end
            k_end = kd[rows_c] * e_end
            decay_tot = jnp.exp(tot)
            for q in range(n_quad):
                sl = slice(q * QUAD, (q + 1) * QUAD)
                vq = v[rows_c, sl]
                setups[(d, c, q)] = chunk_setup(d, a_t[:, sl], r_t[:, sl], b_t[:, sl], k_t[:, sl], vq)
                post[(d, c, q)] = (r_t[:, sl], vq, b_end[:, sl], k_end[:, sl], decay_tot[:, sl])

    keys = list(setups)
    pre = dict(zip(keys, _interleave([setups[key] for key in keys])))
    chains = []
    for d, y_ref in ((0, yf_ref), (1, yr_ref)):
        order = range(n_chunk) if d == 0 else range(n_chunk - 1, -1, -1)
        chains += [state_chain(d, q, order, pre, post, y_ref) for q in range(n_quad)]
    _interleave(chains)
    sfin_ref[...] = s_scr[...]


def _wkv(zbs, kkn, layer, w2p, a2p, w0, a0, ka, rk, bd_ones, s0):
    n_tok = zbs.shape[0]
    t = min(256, n_tok)
    n = n_tok // t
    fwd = lambda i: (i, 0)
    rev = lambda i: (n - 1 - i, 0)
    state_shape = s0.shape
    return pl.pallas_call(
        functools.partial(_wkv_kernel, n_chunk=t // WKV_CHUNK),
        grid=(n,),
        in_specs=[pl.BlockSpec((t, RWKV_IN), fwd), pl.BlockSpec((t, RWKV_IN), rev),
                  pl.BlockSpec((t, WIDTH), fwd), pl.BlockSpec((t, WIDTH), rev),
                  _layer_spec(w2p, layer), _layer_spec(a2p, layer), _layer_spec(w0, layer),
                  _layer_spec(a0, layer), _layer_spec(ka, layer), _layer_spec(rk, layer),
                  _full_spec(bd_ones), _full_spec(s0)],
        out_specs=[pl.BlockSpec((t, WIDTH), fwd), pl.BlockSpec((t, WIDTH), rev),
                   pl.BlockSpec((t, WIDTH), fwd), pl.BlockSpec((t, WIDTH), rev),
                   pl.BlockSpec(state_shape, lambda i: (0, 0, 0, 0))],
        out_shape=[jax.ShapeDtypeStruct((n_tok, WIDTH), F32)] * 4 + [jax.ShapeDtypeStruct(state_shape, F32)],
        scratch_shapes=[pltpu.VMEM(state_shape, F32)],
        compiler_params=_params(),
        name="wkv7_scan",
    )(zbs, zbs, kkn, kkn, w2p, a2p, w0, a0, ka, rk, bd_ones, s0)


def _cmul_rows(x, p):
    half = LANE_TILE // 2
    lane = lax.broadcasted_iota(jnp.int32, p.shape, 1)
    p_sw = pltpu.roll(p, half, 1)
    p1 = jnp.where(lane < half, p, p_sw)[0:1, :]
    p2 = jnp.where(lane < half, -p_sw, p)[0:1, :]
    return x * p1 + pltpu.roll(x, half, 1) * p2


def _s5_kernel(u_ref, w_ref, v_ref, lam_ref, h0_ref, y_ref, fin_ref, *, nc):
    blk = S5_CHUNK * S5_GROUP
    res = _mm(u_ref[...], w_ref[...])
    row = lax.broadcasted_iota(jnp.int32, (nc, LANE_TILE), 0)
    h_in = []
    for d in range(2):
        e = res[:, blk + d * LANE_TILE:blk + (d + 1) * LANE_TILE]
        lam = jnp.broadcast_to(lam_ref[d], (SUBLANE_TILE, LANE_TILE))
        h0 = jnp.broadcast_to(h0_ref[d], (SUBLANE_TILE, LANE_TILE))
        first = 0 if d == 0 else nc - 1
        e = jnp.where(row == first, e + _cmul_rows(h0, lam)[0:1, :], e)
        p = lam
        s = 1
        while s < nc:
            if d == 0:
                sh, m = pltpu.roll(e, s, 0), row >= s
            else:
                sh, m = pltpu.roll(e, nc - s, 0), row < nc - s
            e = e + jnp.where(m, _cmul_rows(sh, p), 0.0)
            p = _cmul_rows(p, p)
            s *= 2
        if d == 0:
            fin_ref[d] = e[nc - 1:nc, :]
            h_in.append(jnp.where(row >= 1, pltpu.roll(e, 1, 0), h0[0:1, :]))
        else:
            fin_ref[d] = e[0:1, :]
            h_in.append(jnp.where(row < nc - 1, pltpu.roll(e, nc - 1, 0), h0[0:1, :]))
    y_ref[...] = res[:, 0:blk] + _mm(jnp.concatenate(h_in, axis=1), v_ref[...])


def _s5(u_blocks, layer, w_cat, v_cat, lam16, h0):
    groups, nc, blk = u_blocks.shape
    kern = functools.partial(_s5_kernel, nc=nc)
    return pl.pallas_call(
        kern,
        grid=(groups,),
        in_specs=[pl.BlockSpec((None, nc, blk), lambda g: (g, 0, 0)),
                  pl.BlockSpec((None, None, blk, w_cat.shape[-1]), lambda g: (layer, g, 0, 0)),
                  pl.BlockSpec((None, None, blk, blk), lambda g: (layer, g, 0, 0)),
                  pl.BlockSpec((None, 2, None, 1, LANE_TILE), lambda g: (layer, 0, g, 0, 0)),
                  pl.BlockSpec((2, None, 1, LANE_TILE), lambda g: (0, g, 0, 0))],
        out_specs=[pl.BlockSpec((None, nc, blk), lambda g: (g, 0, 0)),
                   pl.BlockSpec((2, None, 1, LANE_TILE), lambda g: (0, g, 0, 0))],
        out_shape=[jax.ShapeDtypeStruct((groups, nc, blk), F32),
                   jax.ShapeDtypeStruct((2, groups, 1, LANE_TILE), F32)],
        compiler_params=_params(),
        name="s5_scan",
    )(u_blocks, w_cat, v_cat, lam16, h0)


def _s5_weights(lam_re, lam_im, log_step, b_re, b_im, c_re, c_im):
    n = S5_CHUNK
    step = jnp.exp(log_step)[..., None]
    x_re, ang = lam_re * step, lam_im * step
    mag = jnp.exp(x_re)
    lb_re, lb_im = mag * jnp.cos(ang), mag * jnp.sin(ang)
    nr = jnp.expm1(x_re) * jnp.cos(ang) - 2.0 * jnp.square(jnp.sin(0.5 * ang))
    den = lam_re * lam_re + lam_im * lam_im
    f_re = (nr * lam_re + lb_im * lam_im) / den
    f_im = (lb_im * lam_re - nr * lam_im) / den
    bb_re = f_re[..., None] * b_re - f_im[..., None] * b_im
    bb_im = f_re[..., None] * b_im + f_im[..., None] * b_re
    pr, pi = [jnp.ones_like(lb_re)], [jnp.zeros_like(lb_re)]
    for _ in range(n):
        pr, pi = pr + [pr[-1] * lb_re - pi[-1] * lb_im], pi + [pr[-1] * lb_im + pi[-1] * lb_re]
    pw_re, pw_im = jnp.stack(pr, axis=-2), jnp.stack(pi, axis=-2)
    cp_re = c_re[..., None, :, :] * pw_re[..., :, None, :] - c_im[..., None, :, :] * pw_im[..., :, None, :]
    cp_im = c_re[..., None, :, :] * pw_im[..., :, None, :] + c_im[..., None, :, :] * pw_re[..., :, None, :]
    taps = jnp.sum(cp_re[..., :, None] * bb_re[..., None, None, :, :]
                   - cp_im[..., :, None] * bb_im[..., None, None, :, :], axis=-2)
    fwd, bwd = taps[:, 0], taps[:, 1]
    both = jnp.concatenate([jnp.flip(bwd[..., 1:n, :, :], axis=-3), fwd[..., 0:1, :, :] + bwd[..., 0:1, :, :],
                            fwd[..., 1:n, :, :]], axis=-3)
    lag = jnp.arange(n)[None, :] - jnp.arange(n)[:, None] + (n - 1)
    m_tot = jnp.take(_bf(both), lag.reshape(-1), axis=-3)
    m_tot = m_tot.reshape(both.shape[:-3] + (n, n, S5_GROUP, S5_GROUP))
    m_tot = jnp.swapaxes(jnp.swapaxes(m_tot, -1, -2), -2, -3)
    m_tot = m_tot.reshape(both.shape[:-3] + (n * S5_GROUP, n * S5_GROUP))

    def end_state(d, order):
        qr, qi = pw_re[:, d][..., order, :], pw_im[:, d][..., order, :]
        er = qr[..., :, None, :] * jnp.swapaxes(bb_re[:, d], -1, -2)[..., None, :, :] \
            - qi[..., :, None, :] * jnp.swapaxes(bb_im[:, d], -1, -2)[..., None, :, :]
        ei = qr[..., :, None, :] * jnp.swapaxes(bb_im[:, d], -1, -2)[..., None, :, :] \
            + qi[..., :, None, :] * jnp.swapaxes(bb_re[:, d], -1, -2)[..., None, :, :]
        e = jnp.concatenate([er, ei], axis=-1)
        return e.reshape(e.shape[:-3] + (n * S5_GROUP, 2 * S5_STATE))

    def state_out(d, order):
        vr = jnp.moveaxis(cp_re[:, d][..., order, :, :], -1, -3)
        vi = -jnp.moveaxis(cp_im[:, d][..., order, :, :], -1, -3)
        vv = jnp.concatenate([vr, vi], axis=-3)
        return vv.reshape(vv.shape[:-3] + (2 * S5_STATE, n * S5_GROUP))

    asc = jnp.arange(n)
    w_cat = jnp.concatenate([m_tot, _bf(end_state(0, n - 1 - asc)), _bf(end_state(1, asc))], axis=-1)
    v_cat = jnp.concatenate([state_out(0, asc + 1), state_out(1, n - asc)], axis=-2)
    lam16 = jnp.concatenate([pw_re[..., n, :], pw_im[..., n, :]], axis=-1)[..., None, :]
    return _bf(w_cat), _bf(v_cat), lam16


def _merge_kernel(h_ref, hf_ref, hr_ref, ga_ref, yf_ref, yr_ref, bf_ref, br_ref, gd_ref, ys_ref, uc_ref,
                  mod_ref, g_ref, wzg_ref, lnw_ref, lnb_ref, g2_ref, dsk_ref, wglu_ref, bglu_ref,
                  wbr_ref, wout_ref, bd_ref, o_ref, *, row):
    h = h_ref[...]
    shift = mod_ref[row:row + 1, 0:D]
    scale = mod_ref[row:row + 1, D:2 * D]
    gate = mod_ref[row:row + 1, 2 * D:3 * D]
    zg = _mm(_mod_norm(h, g_ref[...], shift, scale), wzg_ref[...])
    bd = bd_ref[...]
    y_a = jax.nn.gelu(ga_ref[...]) * (hf_ref[...] + hr_ref[...])
    y = yf_ref[...] + yr_ref[...]
    inv_n = 1.0 / HEAD
    yc = y - _head_sum(y, bd) * inv_n
    var = _head_sum(yc * yc, bd) * inv_n
    y_b = yc * lax.rsqrt(var + GN_EPS) * lnw_ref[...] + lnb_ref[...] + (bf_ref[...] + br_ref[...])
    y_b = y_b * _mm(jax.nn.sigmoid(gd_ref[...]), g2_ref[...])
    y_s = jax.nn.gelu(ys_ref[...] + dsk_ref[...] * uc_ref[...])
    y_c = y_s * jax.nn.sigmoid(_mm(y_s, wglu_ref[...]) + bglu_ref[...])
    mix = (jax.nn.sigmoid(zg[:, 0:D]) * _mm(y_a, wbr_ref[0])
           + jax.nn.sigmoid(zg[:, D:2 * D]) * _mm(y_b, wbr_ref[1])
           + jax.nn.sigmoid(zg[:, 2 * D:3 * D]) * _mm(y_c, wbr_ref[2]))
    o_ref[...] = h + gate * _mm(mix, wout_ref[...])


def _merge(h, parts, layer, row, mod, norm1, w_zg, lnw, lnb, g2, dsk, w_glu, b_glu, w_branch, w_out, bd_ones):
    n_tok = h.shape[0]
    tm = min(256, n_tok)
    hf, hr, ga, yf, yr, bf_, br_, zbs, ys5, uc = parts
    kern = functools.partial(_merge_kernel, row=row)
    rs = _row_spec(tm, WIDTH)
    gd_col = (3 * WIDTH + 2 * LANE_TILE) // LANE_TILE
    weights = (mod, norm1, w_zg, lnw, lnb, g2, dsk, w_glu, b_glu, w_branch, w_out)
    return pl.pallas_call(
        kern,
        grid=(n_tok // tm,),
        in_specs=[_row_spec(tm, D), rs, rs, rs, rs, rs, rs, rs, _row_spec(tm, LANE_TILE, gd_col), rs, rs]
        + [_layer_spec(w, layer) for w in weights] + [_full_spec(bd_ones)],
        out_specs=_row_spec(tm, D),
        out_shape=jax.ShapeDtypeStruct((n_tok, D), F32),
        compiler_params=_params(),
        name="merge_project",
    )(h, hf, hr, ga, yf, yr, bf_, br_, zbs, ys5, uc, *weights, bd_ones)


def _ffn_kernel(h_ref, mod_ref, g_ref, win_ref, wout_ref, o_ref, *, row):
    h = h_ref[...]
    shift = mod_ref[row:row + 1, 3 * D:4 * D]
    scale = mod_ref[row:row + 1, 4 * D:5 * D]
    gate = mod_ref[row:row + 1, 5 * D:6 * D]
    gu = _mm(_mod_norm(h, g_ref[...], shift, scale), win_ref[...])
    a, up = gu[:, 0:FFN_HIDDEN], gu[:, FFN_HIDDEN:2 * FFN_HIDDEN]
    o_ref[...] = h + gate * _mm(a * jax.nn.sigmoid(a) * up, wout_ref[...])


def _ffn(h, layer, row, mod, norm2, w_in, w_out):
    n_tok = h.shape[0]
    tm = min(256, n_tok)
    kern = functools.partial(_ffn_kernel, row=row)
    return pl.pallas_call(
        kern,
        grid=(n_tok // tm,),
        in_specs=[_row_spec(tm, D), _layer_spec(mod, layer), _layer_spec(norm2, layer),
                  _layer_spec(w_in, layer), _layer_spec(w_out, layer)],
        out_specs=_row_spec(tm, D),
        out_shape=jax.ShapeDtypeStruct((n_tok, D), F32),
        compiler_params=_params(),
        name="swiglu",
    )(h, mod, norm2, w_in, w_out)


def _final_norm_kernel(h_ref, g_ref, o_ref):
    x = h_ref[...]
    o_ref[...] = x * lax.rsqrt(jnp.mean(x * x, axis=-1, keepdims=True) + EPS) * g_ref[...]


def _final_norm(h, g):
    n_tok = h.shape[0]
    tm = min(512, n_tok)
    return pl.pallas_call(
        _final_norm_kernel,
        grid=(n_tok // tm,),
        in_specs=[_row_spec(tm, D), _full_spec(g)],
        out_specs=_row_spec(tm, D),
        out_shape=jax.ShapeDtypeStruct((n_tok, D), F32),
        compiler_params=_params(),
        name="final_norm",
    )(h, g)


def _block_diag(w):
    n_h, a, b = w.shape[-3:]
    out = w[..., :, :, None, :] * jnp.eye(n_h, dtype=w.dtype)[:, None, :, None]
    return out.reshape(w.shape[:-3] + (n_h * a, n_h * b))


def _pad_lora(w):
    z = jnp.zeros_like(w[:, 0])
    return jnp.stack([jnp.concatenate([w[:, 0], z], axis=1), jnp.concatenate([z, w[:, 1]], axis=1)], axis=1)


def _to_s5_blocks(u, col_major):
    n_tok = u.shape[0]
    if col_major:
        rows = n_tok // GRID_W
        x = u.reshape(rows // S5_CHUNK, S5_CHUNK, GRID_W, S5_GROUPS, S5_GROUP)
        x = x.transpose(3, 2, 0, 1, 4)
    else:
        x = u.reshape(n_tok // S5_CHUNK, S5_CHUNK, S5_GROUPS, S5_GROUP).transpose(2, 0, 1, 3)
    return x.reshape(S5_GROUPS, n_tok // S5_CHUNK, S5_CHUNK * S5_GROUP)


def _from_s5_blocks(y, n_tok, col_major):
    if col_major:
        rows = n_tok // GRID_W
        x = y.reshape(S5_GROUPS, GRID_W, rows // S5_CHUNK, S5_CHUNK, S5_GROUP).transpose(2, 3, 1, 0, 4)
    else:
        x = y.reshape(S5_GROUPS, n_tok // S5_CHUNK, S5_CHUNK, S5_GROUP).transpose(1, 2, 0, 3)
    return x.reshape(n_tok, WIDTH)


def _prepare(c, c_ctx, w_mod, b_mod, norm1, norm2, w_in, lru_conv_w, lru_conv_b, lru_wa, lru_ba, lru_wx, lru_bx,
             lru_lam, rwkv_mu, rwkv_w0, rwkv_w2, rwkv_a0, rwkv_a2, rwkv_g2, rwkv_kk, rwkv_ka, rwkv_rk, rwkv_lnw,
             rwkv_lnb, s5_lam_re, s5_lam_im, s5_log_step, s5_b_re, s5_b_im, s5_c_re, s5_c_im, s5_d, s5_w_glu,
             s5_b_glu, w_branch, w_out, w_ffn_in, w_ffn_out):
    depth = w_in.shape[0]
    vec = lambda a: a.reshape(depth, 1, a.shape[-1])
    cond = jnp.concatenate([c, c_ctx[None], jnp.zeros((SUBLANE_TILE - 2, D), F32)], axis=0)
    w_a, w_zg = _cast_bf16(w_in, ((0, N_A), (N_A, w_in.shape[-1])))
    s5_w, s5_v, s5_lam16 = _s5_weights(s5_lam_re, s5_lam_im, s5_log_step, s5_b_re, s5_b_im, s5_c_re, s5_c_im)
    head_id = jnp.arange(WIDTH) // HEAD
    return dict(
        mod=_modulation(cond, w_mod, b_mod),
        w_a=w_a, w_zg=w_zg,
        w_gate=_bf(jnp.concatenate([_block_diag(lru_wa), _block_diag(lru_wx)], axis=-1)),
        b_gate=jnp.concatenate([lru_ba, lru_bx], axis=-1)[:, :, None, :],
        lam=lru_lam[:, :, None, :],
        w2p=_bf(_pad_lora(rwkv_w2)), a2p=_bf(_pad_lora(rwkv_a2)),
        w0=rwkv_w0[:, :, None, :], a0=rwkv_a0[:, :, None, :], rk=rwkv_rk.reshape(depth, 1, WIDTH),
        s5_w=s5_w, s5_v=s5_v, s5_lam16=s5_lam16,
        w_branch=_cast_bf16(w_branch)[0], w_out=_cast_bf16(w_out)[0],
        w_ffn_in=_cast_bf16(w_ffn_in)[0], w_ffn_out=_cast_bf16(w_ffn_out)[0],
        g2=_bf(rwkv_g2), w_glu=_bf(s5_w_glu), n1=vec(norm1), n2=vec(norm2), conv_w=lru_conv_w,
        conv_b=vec(lru_conv_b), mu=vec(rwkv_mu), kk_w=vec(rwkv_kk), ka=vec(rwkv_ka), lnw=vec(rwkv_lnw),
        lnb=vec(rwkv_lnb), dsk=vec(s5_d), b_glu=vec(s5_b_glu),
        bd_ones=(head_id[:, None] == head_id[None, :]).astype(BF16),
    )


def _zero_states():
    return (jnp.zeros((2, 1, WIDTH), F32),
            jnp.zeros((2, WIDTH // QUAD, QUAD, QUAD), F32),
            jnp.zeros((2, S5_GROUPS, 1, LANE_TILE), F32))


def _mixers(p, h, layer, row, states, col_major):
    n_tok = h.shape[0]
    xc, ga, zbs, uc, kkn = _premix(h, layer, row, p["mod"], p["n1"], p["w_a"], p["conv_w"], p["conv_b"],
                                   p["mu"], p["kk_w"], p["bd_ones"])
    hf, hr, fin_lru = _lru(xc, layer, p["w_gate"], p["b_gate"], p["lam"], states[0])
    yf, yr, bf_, br_, fin_wkv = _wkv(zbs, kkn, layer, p["w2p"], p["a2p"], p["w0"], p["a0"], p["ka"], p["rk"],
                                     p["bd_ones"], states[1])
    ys5, fin_s5 = _s5(_to_s5_blocks(uc, col_major), layer, p["s5_w"], p["s5_v"], p["s5_lam16"], states[2])
    ys5 = _from_s5_blocks(ys5, n_tok, col_major)
    return (hf, hr, ga, yf, yr, bf_, br_, zbs, ys5, uc), (fin_lru, fin_wkv, fin_s5)


def _channel_mix(p, h, parts, layer, row):
    h = _merge(h, parts, layer, row, p["mod"], p["n1"], p["w_zg"], p["lnw"], p["lnb"], p["g2"], p["dsk"],
               p["w_glu"], p["b_glu"], p["w_branch"], p["w_out"], p["bd_ones"])
    return _ffn(h, layer, row, p["mod"], p["n2"], p["w_ffn_in"], p["w_ffn_out"])


def kernel(x, c, ctx, c_ctx, w_mod, b_mod, norm1, norm2, norm_f, w_in, lru_conv_w, lru_conv_b, lru_wa, lru_ba,
           lru_wx, lru_bx, lru_lam, rwkv_mu, rwkv_w0, rwkv_w2, rwkv_a0, rwkv_a2, rwkv_g2, rwkv_kk, rwkv_ka,
           rwkv_rk, rwkv_lnw, rwkv_lnb, s5_lam_re, s5_lam_im, s5_log_step, s5_b_re, s5_b_im, s5_c_re, s5_c_im,
           s5_d, s5_w_glu, s5_b_glu, w_branch, w_out, w_ffn_in, w_ffn_out):
    bsz, n_lat, d_model = x.shape
    n_ctx = ctx.shape[1]
    depth = w_in.shape[0]
    assert bsz == 1 and d_model == D
    assert n_lat % (GRID_W * S5_CHUNK) == 0 and n_lat % 256 == 0
    assert n_ctx % WKV_CHUNK == 0 and (n_ctx <= 256 or n_ctx % 256 == 0)
    p = _prepare(c, c_ctx, w_mod, b_mod, norm1, norm2, w_in, lru_conv_w, lru_conv_b, lru_wa, lru_ba, lru_wx,
                 lru_bx, lru_lam, rwkv_mu, rwkv_w0, rwkv_w2, rwkv_a0, rwkv_a2, rwkv_g2, rwkv_kk, rwkv_ka, rwkv_rk,
                 rwkv_lnw, rwkv_lnb, s5_lam_re, s5_lam_im, s5_log_step, s5_b_re, s5_b_im, s5_c_re, s5_c_im, s5_d,
                 s5_w_glu, s5_b_glu, w_branch, w_out, w_ffn_in, w_ffn_out)
    h_lat, h_ctx = x[0], ctx[0]
    for layer in range(depth):
        parts_c, states = _mixers(p, h_ctx, layer, 1, _zero_states(), False)
        parts_l, _ = _mixers(p, h_lat, layer, 0, states, True)
        h_lat = _channel_mix(p, h_lat, parts_l, layer, 0)
        if layer != depth - 1:
            h_ctx = _channel_mix(p, h_ctx, parts_c, layer, 1)
    return _final_norm(h_lat, norm_f.reshape(1, D))[None]
```

```python
import functools
import math

import jax
import jax.numpy as jnp
from jax import lax
from jax.experimental import pallas as pl
from jax.experimental.pallas import tpu as pltpu

F32 = jnp.float32
BF16 = jnp.bfloat16
HIGHEST = lax.Precision.HIGHEST

D = 1024
WIDTH = 512
GRID_W = 64
EPS = 1e-6
LRU_C = 8.0
HEAD = 64
RWKV_IN = 3 * WIDTH + 2 * 64 + 2 * 64 + 128
GN_EPS = 64e-5
S5_GROUPS = 32
S5_GROUP = 16
S5_STATE = 64
S5_CHUNK = 16
N_A = 2 * WIDTH + RWKV_IN + WIDTH
FFN_HIDDEN = 2816
WKV_CHUNK = 64
LANE_TILE = 128
SUBLANE_TILE = 8
QUAD = 4 * HEAD
VMEM_LIMIT = 56 * 1024 * 1024


def _bf(x):
    return x.astype(BF16)


def _mm(a, b):
    return jnp.dot(_bf(a), _bf(b), preferred_element_type=F32)


def _mm_nt(a, b):
    return lax.dot_general(_bf(a), _bf(b), (((1,), (1,)), ((), ())), preferred_element_type=F32)


def _mm_tn(a, b):
    return lax.dot_general(_bf(a), _bf(b), (((0,), (0,)), ((), ())), preferred_element_type=F32)


def _softplus(x):
    return jnp.maximum(x, 0.0) + jnp.log(1.0 + jnp.exp(-jnp.abs(x)))


def _head_sum(x, bd_ones):
    hi = _bf(x)
    lo = _bf(x - hi.astype(F32))
    return (jnp.dot(hi, bd_ones, preferred_element_type=F32)
            + jnp.dot(lo, bd_ones, preferred_element_type=F32))


def _mod_norm(x, g, shift, scale):
    y = x * lax.rsqrt(jnp.mean(x * x, axis=-1, keepdims=True) + EPS) * g
    return y * (1.0 + scale) + shift


def _params(n_axes=1):
    return pltpu.CompilerParams(dimension_semantics=("arbitrary",) * n_axes,
                                vmem_limit_bytes=VMEM_LIMIT)


def _layer_spec(arr, layer):
    rest = arr.shape[1:]
    return pl.BlockSpec((None,) + rest, lambda *_: (layer,) + (0,) * len(rest))


def _full_spec(arr):
    return pl.BlockSpec(arr.shape, lambda *_: (0,) * arr.ndim)


def _row_spec(tm, width, col=0):
    return pl.BlockSpec((tm, width), lambda i: (i, col))


def _cast_kernel(x_ref, *o_refs, splits):
    for o_ref, (lo, hi) in zip(o_refs, splits):
        o_ref[...] = _bf(x_ref[:, lo:hi])


def _cast_bf16(w, splits=None):
    lead, (rows, cols) = w.shape[:-2], w.shape[-2:]
    splits = splits or ((0, cols),)
    n_rows = math.prod(lead) * rows
    tm = 256
    outs = pl.pallas_call(
        functools.partial(_cast_kernel, splits=splits),
        grid=(n_rows // tm,),
        in_specs=[_row_spec(tm, cols)],
        out_specs=[_row_spec(tm, hi - lo) for lo, hi in splits],
        out_shape=[jax.ShapeDtypeStruct((n_rows, hi - lo), BF16) for lo, hi in splits],
        compiler_params=_params(),
        name="cast_bf16",
    )(w.reshape(n_rows, cols))
    return [o.reshape(lead + (rows, hi - lo)) for o, (lo, hi) in zip(outs, splits)]


def _mod_kernel(cond_ref, w_ref, b_ref, o_ref):
    cnd = cond_ref[...]
    act = cnd * jax.nn.sigmoid(cnd)
    o_ref[...] = jnp.dot(act, w_ref[...], preferred_element_type=F32, precision=HIGHEST) + b_ref[...]


def _modulation(cond, w_mod, b_mod):
    depth = w_mod.shape[0]
    n_col = w_mod.shape[2] // D
    return pl.pallas_call(
        _mod_kernel,
        grid=(depth, n_col),
        in_specs=[pl.BlockSpec((SUBLANE_TILE, D), lambda l, j: (0, 0)),
                  pl.BlockSpec((None, D, D), lambda l, j: (l, 0, j)),
                  pl.BlockSpec((None, 1, D), lambda l, j: (l, 0, j))],
        out_specs=pl.BlockSpec((None, SUBLANE_TILE, D), lambda l, j: (l, 0, j)),
        out_shape=jax.ShapeDtypeStruct((depth, SUBLANE_TILE, 6 * D), F32),
        compiler_params=_params(2),
        name="modulation",
    )(cond, w_mod, b_mod.reshape(depth, 1, 6 * D))


def _premix_kernel(h_ref, hp_ref, hn_ref, mod_ref, g_ref, w_ref, cw_ref, cb_ref, mu_ref, kk_ref, bd_ref,
                   xc_ref, ga_ref, zb_ref, uc_ref, kkn_ref, z_scr, *, row, tm):
    i = pl.program_id(0)
    n = pl.num_programs(0)
    halo = SUBLANE_TILE
    shift = mod_ref[row:row + 1, 0:D]
    scale = mod_ref[row:row + 1, D:2 * D]
    hext = jnp.concatenate([hp_ref[...], h_ref[...], hn_ref[...]], axis=0)
    xn = _mod_norm(hext, g_ref[...], shift, scale)
    z_scr[...] = _mm(xn, w_ref[...])

    @pl.when(i == 0)
    def _():
        z_scr[0:halo, :] = jnp.zeros((halo, N_A), F32)

    @pl.when(i == n - 1)
    def _():
        z_scr[tm + halo:tm + 2 * halo, :] = jnp.zeros((halo, N_A), F32)

    acc = cb_ref[...] + cw_ref[0:1, :] * z_scr[halo - 2:halo - 2 + tm, 0:WIDTH]
    for j in range(1, 4):
        acc = acc + cw_ref[j:j + 1, :] * z_scr[halo - 2 + j:halo - 2 + j + tm, 0:WIDTH]
    xc_ref[...] = acc
    ga_ref[...] = z_scr[halo:halo + tm, WIDTH:2 * WIDTH]
    lo, hi = 2 * WIDTH, 2 * WIDTH + RWKV_IN
    zc = z_scr[halo:halo + tm, lo:hi]
    zp = z_scr[halo - 1:halo - 1 + tm, lo:hi]
    zn = z_scr[halo + 1:halo + 1 + tm, lo:hi]
    zs = zc + mu_ref[...] * (0.5 * (zp + zn) - zc)
    zb_ref[...] = zs
    uc_ref[...] = z_scr[halo:halo + tm, hi:hi + WIDTH]
    kk = zs[:, WIDTH:2 * WIDTH] * kk_ref[...]
    ss = _head_sum(kk * kk, bd_ref[...])
    kkn_ref[...] = kk * lax.rsqrt(ss + 1e-12)


def _premix(h, layer, row, mod, norm1, w_a, conv_w, conv_b, mu, kk_w, bd_ones):
    n_tok = h.shape[0]
    tm = min(256, n_tok)
    n = n_tok // tm
    per = tm // SUBLANE_TILE
    last_blk = n_tok // SUBLANE_TILE - 1
    kern = functools.partial(_premix_kernel, row=row, tm=tm)
    outs = pl.pallas_call(
        kern,
        grid=(n,),
        in_specs=[
            _row_spec(tm, D),
            pl.BlockSpec((SUBLANE_TILE, D), lambda i: (jnp.maximum(i * per - 1, 0), 0)),
            pl.BlockSpec((SUBLANE_TILE, D), lambda i: (jnp.minimum((i + 1) * per, last_blk), 0)),
            _layer_spec(mod, layer), _layer_spec(norm1, layer), _layer_spec(w_a, layer),
            _layer_spec(conv_w, layer), _layer_spec(conv_b, layer), _layer_spec(mu, layer),
            _layer_spec(kk_w, layer), _full_spec(bd_ones),
        ],
        out_specs=[_row_spec(tm, WIDTH), _row_spec(tm, WIDTH), _row_spec(tm, RWKV_IN),
                   _row_spec(tm, WIDTH), _row_spec(tm, WIDTH)],
        out_shape=[jax.ShapeDtypeStruct((n_tok, WIDTH), F32), jax.ShapeDtypeStruct((n_tok, WIDTH), F32),
                   jax.ShapeDtypeStruct((n_tok, RWKV_IN), F32), jax.ShapeDtypeStruct((n_tok, WIDTH), F32),
                   jax.ShapeDtypeStruct((n_tok, WIDTH), F32)],
        scratch_shapes=[pltpu.VMEM((tm + 2 * SUBLANE_TILE, N_A), F32)],
        compiler_params=_params(),
        name="premix",
    )(h, h, h, mod, norm1, w_a, conv_w, conv_b, mu, kk_w, bd_ones)
    return outs


def _lru_kernel(xf_ref, xr_ref, wg_ref, bg_ref, lam_ref, h0_ref, hf_ref, hr_ref, fin_ref, carry, *, tm):
    i = pl.program_id(0)

    @pl.when(i == 0)
    def _():
        carry[...] = h0_ref[...]

    rows = lax.broadcasted_iota(jnp.int32, (tm, WIDTH), 0)
    for d, (x_ref, o_ref) in enumerate(((xf_ref, hf_ref), (xr_ref, hr_ref))):
        xc = x_ref[...]
        gates = _mm(xc, wg_ref[d]) + bg_ref[d]
        gate_r = jax.nn.sigmoid(gates[:, 0:WIDTH])
        gate_i = jax.nn.sigmoid(gates[:, WIDTH:2 * WIDTH])
        log_a = -LRU_C * gate_r * _softplus(-lam_ref[d])
        a = jnp.exp(log_a)
        b = jnp.sqrt(1.0 - jnp.exp(2.0 * log_a)) * (gate_i * xc)
        s = 1
        while s < tm:
            if d == 0:
                a_sh, b_sh, m = pltpu.roll(a, s, 0), pltpu.roll(b, s, 0), rows >= s
            else:
                a_sh, b_sh, m = pltpu.roll(a, tm - s, 0), pltpu.roll(b, tm - s, 0), rows < tm - s
            b = jnp.where(m, a * b_sh + b, b)
            a = jnp.where(m, a * a_sh, a)
            s *= 2
        hs = a * carry[d] + b
        o_ref[...] = hs
        carry[d] = hs[tm - 1:tm, :] if d == 0 else hs[0:1, :]
    fin_ref[...] = carry[...]


def _lru(xc, layer, w_gate, b_gate, lam, h0):
    n_tok = xc.shape[0]
    tm = min(256, n_tok)
    n = n_tok // tm
    kern = functools.partial(_lru_kernel, tm=tm)
    return pl.pallas_call(
        kern,
        grid=(n,),
        in_specs=[_row_spec(tm, WIDTH),
                  pl.BlockSpec((tm, WIDTH), lambda i: (n - 1 - i, 0)),
                  _layer_spec(w_gate, layer), _layer_spec(b_gate, layer), _layer_spec(lam, layer),
                  _full_spec(h0)],
        out_specs=[_row_spec(tm, WIDTH),
                   pl.BlockSpec((tm, WIDTH), lambda i: (n - 1 - i, 0)),
                   pl.BlockSpec((2, 1, WIDTH), lambda i: (0, 0, 0))],
        out_shape=[jax.ShapeDtypeStruct((n_tok, WIDTH), F32), jax.ShapeDtypeStruct((n_tok, WIDTH), F32),
                   jax.ShapeDtypeStruct((2, 1, WIDTH), F32)],
        scratch_shapes=[pltpu.VMEM((2, 1, WIDTH), F32)],
        compiler_params=_params(),
        name="rglru_scan",
    )(xc, xc, w_gate, b_gate, lam, h0)


def _block_rows(x, lane_head):
    return jnp.concatenate([jnp.where(lane_head == h, x, 0.0) for h in range(QUAD // HEAD)], axis=0)


def _interleave(gens):
    results = [None] * len(gens)
    active = list(range(len(gens)))
    while active:
        still = []
        for g in active:
            try:
                next(gens[g])
                still.append(g)
            except StopIteration as stop:
                results[g] = stop.value
        active = still
    return results


def _wkv_kernel(zf_ref, zr_ref, kf_ref, kr_ref, w2_ref, a2_ref, w0_ref, a0_ref, ka_ref, rk_ref, bd_ref, s0_ref,
                yf_ref, yr_ref, bf_ref, br_ref, sfin_ref, s_scr, *, n_chunk):
    i = pl.program_id(0)
    t = WKV_CHUNK

    @pl.when(i == 0)
    def _():
        s_scr[...] = s0_ref[...]

    row = lax.broadcasted_iota(jnp.int32, (t, QUAD), 0)
    lane = lax.broadcasted_iota(jnp.int32, (t, QUAD), 1)
    lane_head = lane >> 6
    lane_tok = lane & (HEAD - 1)
    row_t = lax.broadcasted_iota(jnp.int32, (t, t), 0)
    col_t = lax.broadcasted_iota(jnp.int32, (t, t), 1)
    same_head = (lax.broadcasted_iota(jnp.int32, (QUAD, QUAD), 0) >> 6) == (
        lax.broadcasted_iota(jnp.int32, (QUAD, QUAD), 1) >> 6)
    eye = jnp.where(lane_tok == row, 1.0, 0.0)
    pair_mask = [((row ^ lane_tok) >> lvl) == 1 for lvl in range(int(math.log2(t)))]

    n_quad = WIDTH // QUAD
    bd = functools.partial(_block_rows, lane_head=lane_head)

    def chunk_setup(d, at, rt, bt, kt, vq):
        strict, incl = (lane_tok < row, lane_tok <= row) if d == 0 else (lane_tok > row, lane_tok >= row)
        ar = jnp.concatenate([at, rt], axis=0)
        gb = _mm_nt(ar, bd(bt))
        yield
        gk = _mm_nt(ar, bd(kt))
        yield
        a_ab = jnp.where(strict, gb[0:t], 0.0)
        a_rb = jnp.where(incl, gb[t:2 * t], 0.0)
        a_ak = jnp.where(strict, gk[0:t], 0.0)
        a_rk = jnp.where(incl, gk[t:2 * t], 0.0)
        inv = eye + jnp.where(pair_mask[0], a_ab, 0.0)
        for lvl in range(1, len(pair_mask)):
            half = _mm(inv, bd(jnp.where(pair_mask[lvl], a_ab, 0.0)))
            yield
            inv = inv + _mm(half, bd(inv))
            yield
        v_bd = bd(vq)
        xy0 = _mm(jnp.concatenate([a_ak, a_rk], axis=0), v_bd)
        yield
        x0, y0 = xy0[0:t], xy0[t:2 * t]
        wu = _mm(inv, jnp.concatenate([bd(at), bd(x0)], axis=1))
        yield
        return wu[:, 0:QUAD], wu[:, QUAD:2 * QUAD], a_rb, y0

    def state_chain(d, q, order, pre, post, y_ref):
        s = s_scr[d, q]
        for c in order:
            w1, u0, a_rb, y0 = pre[(d, c, q)]
            rt, vq, b_end, k_end, decay = post[(d, c, q)]
            ws = _mm_nt(jnp.concatenate([w1, rt], axis=0), s)
            yield
            u = ws[0:t] + u0
            y = ws[t:2 * t] + y0 + _mm(a_rb, bd(u))
            yield
            upd = _mm_tn(jnp.concatenate([u, vq], axis=0), jnp.concatenate([b_end, k_end], axis=0))
            yield
            s = s * decay + jnp.where(same_head, upd, 0.0)
            y_ref[c * t:(c + 1) * t, q * QUAD:(q + 1) * QUAD] = y
        s_scr[d, q] = s

    setups, post = {}, {}
    dirs = ((zf_ref, kf_ref, yf_ref, bf_ref), (zr_ref, kr_ref, yr_ref, br_ref))
    for d, (z_ref, k_ref, y_ref, b_ref) in enumerate(dirs):
        r = z_ref[:, 0:WIDTH]
        k = z_ref[:, WIDTH:2 * WIDTH]
        v = z_ref[:, 2 * WIDTH:3 * WIDTH]
        wd = z_ref[:, 3 * WIDTH:3 * WIDTH + LANE_TILE]
        ad = z_ref[:, 3 * WIDTH + LANE_TILE:3 * WIDTH + 2 * LANE_TILE]
        kkn = k_ref[...]
        w_log = -_softplus(-(w0_ref[d] + _mm(jnp.tanh(wd), w2_ref[d]))) - 0.5
        lw = -jnp.exp(w_log)
        iclr = jax.nn.sigmoid(a0_ref[d] + _mm(ad, a2_ref[d]))
        kd = k * (1.0 + (iclr - 1.0) * ka_ref[...])
        kb = kkn * iclr
        b_ref[...] = _mm(r * kd * rk_ref[...], bd_ref[...]) * v
        p1 = _bf(lw)
        p2 = _bf(lw - p1.astype(F32))
        lw2 = jnp.concatenate([p1, p2], axis=1)
        tri = jnp.where(col_t <= row_t if d == 0 else col_t >= row_t, 1.0, 0.0).astype(BF16)
        for c in range(n_chunk):
            rows_c = slice(c * t, (c + 1) * t)
            cum2 = jnp.dot(tri, lw2[rows_c], preferred_element_type=F32)
            cum = cum2[:, 0:WIDTH] + cum2[:, WIDTH:2 * WIDTH]
            tot = cum[t - 1:t, :] if d == 0 else cum[0:1, :]
            e_neg = jnp.exp(-cum)
            e_end = jnp.exp(tot - cum)
            a_t = -kkn[rows_c] * jnp.exp(cum - lw[rows_c])
            r_t = r[rows_c] * jnp.exp(cum)
            b_t = kb[rows_c] * e_neg
            k_t = kd[rows_c] * e_neg
            b_end = kb[rows_c] * e_end
            k_end = kd[rows_c] * e_end
            decay_tot = jnp.exp(tot)
            for q in range(n_quad):
                sl = slice(q * QUAD, (q + 1) * QUAD)
                vq = v[rows_c, sl]
                setups[(d, c, q)] = chunk_setup(d, a_t[:, sl], r_t[:, sl], b_t[:, sl], k_t[:, sl], vq)
                post[(d, c, q)] = (r_t[:, sl], vq, b_end[:, sl], k_end[:, sl], decay_tot[:, sl])

    keys = list(setups)
    pre = dict(zip(keys, _interleave([setups[key] for key in keys])))
    chains = []
    for d, y_ref in ((0, yf_ref), (1, yr_ref)):
        order = range(n_chunk) if d == 0 else range(n_chunk - 1, -1, -1)
        chains += [state_chain(d, q, order, pre, post, y_ref) for q in range(n_quad)]
    _interleave(chains)
    sfin_ref[...] = s_scr[...]


def _wkv(zbs, kkn, layer, w2p, a2p, w0, a0, ka, rk, bd_ones, s0):
    n_tok = zbs.shape[0]
    t = min(256, n_tok)
    n = n_tok // t
    fwd = lambda i: (i, 0)
    rev = lambda i: (n - 1 - i, 0)
    state_shape = s0.shape
    return pl.pallas_call(
        functools.partial(_wkv_kernel, n_chunk=t // WKV_CHUNK),
        grid=(n,),
        in_specs=[pl.BlockSpec((t, RWKV_IN), fwd), pl.BlockSpec((t, RWKV_IN), rev),
                  pl.BlockSpec((t, WIDTH), fwd), pl.BlockSpec((t, WIDTH), rev),
                  _layer_spec(w2p, layer), _layer_spec(a2p, layer), _layer_spec(w0, layer),
                  _layer_spec(a0, layer), _layer_spec(ka, layer), _layer_spec(rk, layer),
                  _full_spec(bd_ones), _full_spec(s0)],
        out_specs=[pl.BlockSpec((t, WIDTH), fwd), pl.BlockSpec((t, WIDTH), rev),
                   pl.BlockSpec((t, WIDTH), fwd), pl.BlockSpec((t, WIDTH), rev),
                   pl.BlockSpec(state_shape, lambda i: (0, 0, 0, 0))],
        out_shape=[jax.ShapeDtypeStruct((n_tok, WIDTH), F32)] * 4 + [jax.ShapeDtypeStruct(state_shape, F32)],
        scratch_shapes=[pltpu.VMEM(state_shape, F32)],
        compiler_params=_params(),
        name="wkv7_scan",
    )(zbs, zbs, kkn, kkn, w2p, a2p, w0, a0, ka, rk, bd_ones, s0)


def _cmul_rows(x, p):
    half = LANE_TILE // 2
    lane = lax.broadcasted_iota(jnp.int32, p.shape, 1)
    p_sw = pltpu.roll(p, half, 1)
    p1 = jnp.where(lane < half, p, p_sw)[0:1, :]
    p2 = jnp.where(lane < half, -p_sw, p)[0:1, :]
    return x * p1 + pltpu.roll(x, half, 1) * p2


def _s5_kernel(u_ref, w_ref, v_ref, lam_ref, h0_ref, y_ref, fin_ref, *, nc):
    blk = S5_CHUNK * S5_GROUP
    res = _mm(u_ref[...], w_ref[...])
    row = lax.broadcasted_iota(jnp.int32, (nc, LANE_TILE), 0)
    h_in = []
    for d in range(2):
        e = res[:, blk + d * LANE_TILE:blk + (d + 1) * LANE_TILE]
        lam = jnp.broadcast_to(lam_ref[d], (SUBLANE_TILE, LANE_TILE))
        h0 = jnp.broadcast_to(h0_ref[d], (SUBLANE_TILE, LANE_TILE))
        first = 0 if d == 0 else nc - 1
        e = jnp.where(row == first, e + _cmul_rows(h0, lam)[0:1, :], e)
        p = lam
        s = 1
        while s < nc:
            if d == 0:
                sh, m = pltpu.roll(e, s, 0), row >= s
            else:
                sh, m = pltpu.roll(e, nc - s, 0), row < nc - s
            e = e + jnp.where(m, _cmul_rows(sh, p), 0.0)
            p = _cmul_rows(p, p)
            s *= 2
        if d == 0:
            fin_ref[d] = e[nc - 1:nc, :]
            h_in.append(jnp.where(row >= 1, pltpu.roll(e, 1, 0), h0[0:1, :]))
        else:
            fin_ref[d] = e[0:1, :]
            h_in.append(jnp.where(row < nc - 1, pltpu.roll(e, nc - 1, 0), h0[0:1, :]))
    y_ref[...] = res[:, 0:blk] + _mm(jnp.concatenate(h_in, axis=1), v_ref[...])


def _s5(u_blocks, layer, w_cat, v_cat, lam16, h0):
    groups, nc, blk = u_blocks.shape
    kern = functools.partial(_s5_kernel, nc=nc)
    return pl.pallas_call(
        kern,
        grid=(groups,),
        in_specs=[pl.BlockSpec((None, nc, blk), lambda g: (g, 0, 0)),
                  pl.BlockSpec((None, None, blk, w_cat.shape[-1]), lambda g: (layer, g, 0, 0)),
                  pl.BlockSpec((None, None, blk, blk), lambda g: (layer, g, 0, 0)),
                  pl.BlockSpec((None, 2, None, 1, LANE_TILE), lambda g: (layer, 0, g, 0, 0)),
                  pl.BlockSpec((2, None, 1, LANE_TILE), lambda g: (0, g, 0, 0))],
        out_specs=[pl.BlockSpec((None, nc, blk), lambda g: (g, 0, 0)),
                   pl.BlockSpec((2, None, 1, LANE_TILE), lambda g: (0, g, 0, 0))],
        out_shape=[jax.ShapeDtypeStruct((groups, nc, blk), F32),
                   jax.ShapeDtypeStruct((2, groups, 1, LANE_TILE), F32)],
        compiler_params=_params(),
        name="s5_scan",
    )(u_blocks, w_cat, v_cat, lam16, h0)


def _s5_weights(lam_re, lam_im, log_step, b_re, b_im, c_re, c_im):
    n = S5_CHUNK
    step = jnp.exp(log_step)[..., None]
    x_re, ang = lam_re * step, lam_im * step
    mag = jnp.exp(x_re)
    lb_re, lb_im = mag * jnp.cos(ang), mag * jnp.sin(ang)
    nr = jnp.expm1(x_re) * jnp.cos(ang) - 2.0 * jnp.square(jnp.sin(0.5 * ang))
    den = lam_re * lam_re + lam_im * lam_im
    f_re = (nr * lam_re + lb_im * lam_im) / den
    f_im = (lb_im * lam_re - nr * lam_im) / den
    bb_re = f_re[..., None] * b_re - f_im[..., None] * b_im
    bb_im = f_re[..., None] * b_im + f_im[..., None] * b_re
    pr, pi = [jnp.ones_like(lb_re)], [jnp.zeros_like(lb_re)]
    for _ in range(n):
        pr, pi = pr + [pr[-1] * lb_re - pi[-1] * lb_im], pi + [pr[-1] * lb_im + pi[-1] * lb_re]
    pw_re, pw_im = jnp.stack(pr, axis=-2), jnp.stack(pi, axis=-2)
    cp_re = c_re[..., None, :, :] * pw_re[..., :, None, :] - c_im[..., None, :, :] * pw_im[..., :, None, :]
    cp_im = c_re[..., None, :, :] * pw_im[..., :, None, :] + c_im[..., None, :, :] * pw_re[..., :, None, :]
    taps = jnp.sum(cp_re[..., :, None] * bb_re[..., None, None, :, :]
                   - cp_im[..., :, None] * bb_im[..., None, None, :, :], axis=-2)
    fwd, bwd = taps[:, 0], taps[:, 1]
    both = jnp.concatenate([jnp.flip(bwd[..., 1:n, :, :], axis=-3), fwd[..., 0:1, :, :] + bwd[..., 0:1, :, :],
                            fwd[..., 1:n, :, :]], axis=-3)
    lag = jnp.arange(n)[None, :] - jnp.arange(n)[:, None] + (n - 1)
    m_tot = jnp.take(_bf(both), lag.reshape(-1), axis=-3)
    m_tot = m_tot.reshape(both.shape[:-3] + (n, n, S5_GROUP, S5_GROUP))
    m_tot = jnp.swapaxes(jnp.swapaxes(m_tot, -1, -2), -2, -3)
    m_tot = m_tot.reshape(both.shape[:-3] + (n * S5_GROUP, n * S5_GROUP))

    def end_state(d, order):
        qr, qi = pw_re[:, d][..., order, :], pw_im[:, d][..., order, :]
        er = qr[..., :, None, :] * jnp.swapaxes(bb_re[:, d], -1, -2)[..., None, :, :] \
            - qi[..., :, None, :] * jnp.swapaxes(bb_im[:, d], -1, -2)[..., None, :, :]
        ei = qr[..., :, None, :] * jnp.swapaxes(bb_im[:, d], -1, -2)[..., None, :, :] \
            + qi[..., :, None, :] * jnp.swapaxes(bb_re[:, d], -1, -2)[..., None, :, :]
        e = jnp.concatenate([er, ei], axis=-1)
        return e.reshape(e.shape[:-3] + (n * S5_GROUP, 2 * S5_STATE))

    def state_out(d, order):
        vr = jnp.moveaxis(cp_re[:, d][..., order, :, :], -1, -3)
        vi = -jnp.moveaxis(cp_im[:, d][..., order, :, :], -1, -3)
        vv = jnp.concatenate([vr, vi], axis=-3)
        return vv.reshape(vv.shape[:-3] + (2 * S5_STATE, n * S5_GROUP))

    asc = jnp.arange(n)
    w_cat = jnp.concatenate([m_tot, _bf(end_state(0, n - 1 - asc)), _bf(end_state(1, asc))], axis=-1)
    v_cat = jnp.concatenate([state_out(0, asc + 1), state_out(1, n - asc)], axis=-2)
    lam16 = jnp.concatenate([pw_re[..., n, :], pw_im[..., n, :]], axis=-1)[..., None, :]
    return _bf(w_cat), _bf(v_cat), lam16


def _merge_kernel(h_ref, hf_ref, hr_ref, ga_ref, yf_ref, yr_ref, bf_ref, br_ref, gd_ref, ys_ref, uc_ref,
                  mod_ref, g_ref, wzg_ref, lnw_ref, lnb_ref, g2_ref, dsk_ref, wglu_ref, bglu_ref,
                  wbr_ref, wout_ref, bd_ref, o_ref, *, row):
    h = h_ref[...]
    shift = mod_ref[row:row + 1, 0:D]
    scale = mod_ref[row:row + 1, D:2 * D]
    gate = mod_ref[row:row + 1, 2 * D:3 * D]
    zg = _mm(_mod_norm(h, g_ref[...], shift, scale), wzg_ref[...])
    bd = bd_ref[...]
    y_a = jax.nn.gelu(ga_ref[...]) * (hf_ref[...] + hr_ref[...])
    y = yf_ref[...] + yr_ref[...]
    inv_n = 1.0 / HEAD
    yc = y - _mm(y, bd) * inv_n
    var = _mm(yc * yc, bd) * inv_n
    y_b = yc * lax.rsqrt(var + GN_EPS) * lnw_ref[...] + lnb_ref[...] + (bf_ref[...] + br_ref[...])
    y_b = y_b * _mm(jax.nn.sigmoid(gd_ref[...]), g2_ref[...])
    y_s = jax.nn.gelu(ys_ref[...] + dsk_ref[...] * uc_ref[...])
    y_c = y_s * jax.nn.sigmoid(_mm(y_s, wglu_ref[...]) + bglu_ref[...])
    mix = (jax.nn.sigmoid(zg[:, 0:D]) * _mm(y_a, wbr_ref[0])
           + jax.nn.sigmoid(zg[:, D:2 * D]) * _mm(y_b, wbr_ref[1])
           + jax.nn.sigmoid(zg[:, 2 * D:3 * D]) * _mm(y_c, wbr_ref[2]))
    o_ref[...] = h + gate * _mm(mix, wout_ref[...])


def _merge(h, parts, layer, row, mod, norm1, w_zg, lnw, lnb, g2, dsk, w_glu, b_glu, w_branch, w_out, bd_ones):
    n_tok = h.shape[0]
    tm = min(256, n_tok)
    hf, hr, ga, yf, yr, bf_, br_, zbs, ys5, uc = parts
    kern = functools.partial(_merge_kernel, row=row)
    rs = _row_spec(tm, WIDTH)
    gd_col = (3 * WIDTH + 2 * LANE_TILE) // LANE_TILE
    weights = (mod, norm1, w_zg, lnw, lnb, g2, dsk, w_glu, b_glu, w_branch, w_out)
    return pl.pallas_call(
        kern,
        grid=(n_tok // tm,),
        in_specs=[_row_spec(tm, D), rs, rs, rs, rs, rs, rs, rs, _row_spec(tm, LANE_TILE, gd_col), rs, rs]
        + [_layer_spec(w, layer) for w in weights] + [_full_spec(bd_ones)],
        out_specs=_row_spec(tm, D),
        out_shape=jax.ShapeDtypeStruct((n_tok, D), F32),
        compiler_params=_params(),
        name="merge_project",
    )(h, hf, hr, ga, yf, yr, bf_, br_, zbs, ys5, uc, *weights, bd_ones)


def _ffn_kernel(h_ref, mod_ref, g_ref, win_ref, wout_ref, o_ref, *, row):
    h = h_ref[...]
    shift = mod_ref[row:row + 1, 3 * D:4 * D]
    scale = mod_ref[row:row + 1, 4 * D:5 * D]
    gate = mod_ref[row:row + 1, 5 * D:6 * D]
    gu = _mm(_mod_norm(h, g_ref[...], shift, scale), win_ref[...])
    a, up = gu[:, 0:FFN_HIDDEN], gu[:, FFN_HIDDEN:2 * FFN_HIDDEN]
    o_ref[...] = h + gate * _mm(a * jax.nn.sigmoid(a) * up, wout_ref[...])


def _ffn(h, layer, row, mod, norm2, w_in, w_out):
    n_tok = h.shape[0]
    tm = min(256, n_tok)
    kern = functools.partial(_ffn_kernel, row=row)
    return pl.pallas_call(
        kern,
        grid=(n_tok // tm,),
        in_specs=[_row_spec(tm, D), _layer_spec(mod, layer), _layer_spec(norm2, layer),
                  _layer_spec(w_in, layer), _layer_spec(w_out, layer)],
        out_specs=_row_spec(tm, D),
        out_shape=jax.ShapeDtypeStruct((n_tok, D), F32),
        compiler_params=_params(),
        name="swiglu",
    )(h, mod, norm2, w_in, w_out)


def _final_norm_kernel(h_ref, g_ref, o_ref):
    x = h_ref[...]
    o_ref[...] = x * lax.rsqrt(jnp.mean(x * x, axis=-1, keepdims=True) + EPS) * g_ref[...]


def _final_norm(h, g):
    n_tok = h.shape[0]
    tm = min(512, n_tok)
    return pl.pallas_call(
        _final_norm_kernel,
        grid=(n_tok // tm,),
        in_specs=[_row_spec(tm, D), _full_spec(g)],
        out_specs=_row_spec(tm, D),
        out_shape=jax.ShapeDtypeStruct((n_tok, D), F32),
        compiler_params=_params(),
        name="final_norm",
    )(h, g)


def _block_diag(w):
    n_h, a, b = w.shape[-3:]
    out = w[..., :, :, None, :] * jnp.eye(n_h, dtype=w.dtype)[:, None, :, None]
    return out.reshape(w.shape[:-3] + (n_h * a, n_h * b))


def _pad_lora(w):
    z = jnp.zeros_like(w[:, 0])
    return jnp.stack([jnp.concatenate([w[:, 0], z], axis=1), jnp.concatenate([z, w[:, 1]], axis=1)], axis=1)


def _to_s5_blocks(u, col_major):
    n_tok = u.shape[0]
    if col_major:
        rows = n_tok // GRID_W
        x = u.reshape(rows // S5_CHUNK, S5_CHUNK, GRID_W, S5_GROUPS, S5_GROUP)
        x = x.transpose(3, 2, 0, 1, 4)
    else:
        x = u.reshape(n_tok // S5_CHUNK, S5_CHUNK, S5_GROUPS, S5_GROUP).transpose(2, 0, 1, 3)
    return x.reshape(S5_GROUPS, n_tok // S5_CHUNK, S5_CHUNK * S5_GROUP)


def _from_s5_blocks(y, n_tok, col_major):
    if col_major:
        rows = n_tok // GRID_W
        x = y.reshape(S5_GROUPS, GRID_W, rows // S5_CHUNK, S5_CHUNK, S5_GROUP).transpose(2, 3, 1, 0, 4)
    else:
        x = y.reshape(S5_GROUPS, n_tok // S5_CHUNK, S5_CHUNK, S5_GROUP).transpose(1, 2, 0, 3)
    return x.reshape(n_tok, WIDTH)


def _prepare(c, c_ctx, w_mod, b_mod, norm1, norm2, w_in, lru_conv_w, lru_conv_b, lru_wa, lru_ba, lru_wx, lru_bx,
             lru_lam, rwkv_mu, rwkv_w0, rwkv_w2, rwkv_a0, rwkv_a2, rwkv_g2, rwkv_kk, rwkv_ka, rwkv_rk, rwkv_lnw,
             rwkv_lnb, s5_lam_re, s5_lam_im, s5_log_step, s5_b_re, s5_b_im, s5_c_re, s5_c_im, s5_d, s5_w_glu,
             s5_b_glu, w_branch, w_out, w_ffn_in, w_ffn_out):
    depth = w_in.shape[0]
    vec = lambda a: a.reshape(depth, 1, a.shape[-1])
    cond = jnp.concatenate([c, c_ctx[None], jnp.zeros((SUBLANE_TILE - 2, D), F32)], axis=0)
    w_a, w_zg = _cast_bf16(w_in, ((0, N_A), (N_A, w_in.shape[-1])))
    s5_w, s5_v, s5_lam16 = _s5_weights(s5_lam_re, s5_lam_im, s5_log_step, s5_b_re, s5_b_im, s5_c_re, s5_c_im)
    head_id = jnp.arange(WIDTH) // HEAD
    return dict(
        mod=_modulation(cond, w_mod, b_mod),
        w_a=w_a, w_zg=w_zg,
        w_gate=_bf(jnp.concatenate([_block_diag(lru_wa), _block_diag(lru_wx)], axis=-1)),
        b_gate=jnp.concatenate([lru_ba, lru_bx], axis=-1)[:, :, None, :],
        lam=lru_lam[:, :, None, :],
        w2p=_bf(_pad_lora(rwkv_w2)), a2p=_bf(_pad_lora(rwkv_a2)),
        w0=rwkv_w0[:, :, None, :], a0=rwkv_a0[:, :, None, :], rk=rwkv_rk.reshape(depth, 1, WIDTH),
        s5_w=s5_w, s5_v=s5_v, s5_lam16=s5_lam16,
        w_branch=_cast_bf16(w_branch)[0], w_out=_cast_bf16(w_out)[0],
        w_ffn_in=_cast_bf16(w_ffn_in)[0], w_ffn_out=_cast_bf16(w_ffn_out)[0],
        g2=_bf(rwkv_g2), w_glu=_bf(s5_w_glu), n1=vec(norm1), n2=vec(norm2), conv_w=lru_conv_w,
        conv_b=vec(lru_conv_b), mu=vec(rwkv_mu), kk_w=vec(rwkv_kk), ka=vec(rwkv_ka), lnw=vec(rwkv_lnw),
        lnb=vec(rwkv_lnb), dsk=vec(s5_d), b_glu=vec(s5_b_glu),
        bd_ones=(head_id[:, None] == head_id[None, :]).astype(BF16),
    )


def _zero_states():
    return (jnp.zeros((2, 1, WIDTH), F32),
            jnp.zeros((2, WIDTH // QUAD, QUAD, QUAD), F32),
            jnp.zeros((2, S5_GROUPS, 1, LANE_TILE), F32))


def _mixers(p, h, layer, row, states, col_major):
    n_tok = h.shape[0]
    xc, ga, zbs, uc, kkn = _premix(h, layer, row, p["mod"], p["n1"], p["w_a"], p["conv_w"], p["conv_b"],
                                   p["mu"], p["kk_w"], p["bd_ones"])
    hf, hr, fin_lru = _lru(xc, layer, p["w_gate"], p["b_gate"], p["lam"], states[0])
    yf, yr, bf_, br_, fin_wkv = _wkv(zbs, kkn, layer, p["w2p"], p["a2p"], p["w0"], p["a0"], p["ka"], p["rk"],
                                     p["bd_ones"], states[1])
    ys5, fin_s5 = _s5(_to_s5_blocks(_bf(uc), col_major), layer, p["s5_w"], p["s5_v"], p["s5_lam16"], states[2])
    ys5 = _from_s5_blocks(ys5, n_tok, col_major)
    return (hf, hr, ga, yf, yr, bf_, br_, zbs, ys5, uc), (fin_lru, fin_wkv, fin_s5)


def _channel_mix(p, h, parts, layer, row):
    h = _merge(h, parts, layer, row, p["mod"], p["n1"], p["w_zg"], p["lnw"], p["lnb"], p["g2"], p["dsk"],
               p["w_glu"], p["b_glu"], p["w_branch"], p["w_out"], p["bd_ones"])
    return _ffn(h, layer, row, p["mod"], p["n2"], p["w_ffn_in"], p["w_ffn_out"])


def kernel(x, c, ctx, c_ctx, w_mod, b_mod, norm1, norm2, norm_f, w_in, lru_conv_w, lru_conv_b, lru_wa, lru_ba,
           lru_wx, lru_bx, lru_lam, rwkv_mu, rwkv_w0, rwkv_w2, rwkv_a0, rwkv_a2, rwkv_g2, rwkv_kk, rwkv_ka,
           rwkv_rk, rwkv_lnw, rwkv_lnb, s5_lam_re, s5_lam_im, s5_log_step, s5_b_re, s5_b_im, s5_c_re, s5_c_im,
           s5_d, s5_w_glu, s5_b_glu, w_branch, w_out, w_ffn_in, w_ffn_out):
    bsz, n_lat, d_model = x.shape
    n_ctx = ctx.shape[1]
    depth = w_in.shape[0]
    assert bsz == 1 and d_model == D
    assert n_lat % (GRID_W * S5_CHUNK) == 0 and n_lat % 256 == 0
    assert n_ctx % WKV_CHUNK == 0 and (n_ctx <= 256 or n_ctx % 256 == 0)
    p = _prepare(c, c_ctx, w_mod, b_mod, norm1, norm2, w_in, lru_conv_w, lru_conv_b, lru_wa, lru_ba, lru_wx,
                 lru_bx, lru_lam, rwkv_mu, rwkv_w0, rwkv_w2, rwkv_a0, rwkv_a2, rwkv_g2, rwkv_kk, rwkv_ka, rwkv_rk,
                 rwkv_lnw, rwkv_lnb, s5_lam_re, s5_lam_im, s5_log_step, s5_b_re, s5_b_im, s5_c_re, s5_c_im, s5_d,
                 s5_w_glu, s5_b_glu, w_branch, w_out, w_ffn_in, w_ffn_out)
    h_lat, h_ctx = x[0], ctx[0]
    for layer in range(depth):
        parts_c, states = _mixers(p, h_ctx, layer, 1, _zero_states(), False)
        parts_l, _ = _mixers(p, h_lat, layer, 0, states, True)
        h_lat = _channel_mix(p, h_lat, parts_l, layer, 0)
        if layer != depth - 1:
            h_ctx = _channel_mix(p, h_ctx, parts_c, layer, 1)
    return _final_norm(h_lat, norm_f.reshape(1, D))[None]
```

```python
import functools
import math

import jax
import jax.numpy as jnp
from jax import lax
from jax.experimental import pallas as pl
from jax.experimental.pallas import tpu as pltpu

F32 = jnp.float32
BF16 = jnp.bfloat16
HIGHEST = lax.Precision.HIGHEST

D = 1024
WIDTH = 512
GRID_W = 64
EPS = 1e-6
LRU_C = 8.0
HEAD = 64
RWKV_IN = 3 * WIDTH + 2 * 64 + 2 * 64 + 128
GN_EPS = 64e-5
S5_GROUPS = 32
S5_GROUP = 16
S5_STATE = 64
S5_CHUNK = 16
N_A = 2 * WIDTH + RWKV_IN + WIDTH
FFN_HIDDEN = 2816
WKV_CHUNK = 64
LANE_TILE = 128
SUBLANE_TILE = 8
QUAD = 4 * HEAD
VMEM_LIMIT = 56 * 1024 * 1024


def _bf(x):
    return x.astype(BF16)


def _mm(a, b):
    return jnp.dot(_bf(a), _bf(b), preferred_element_type=F32)


def _mm_nt(a, b):
    return lax.dot_general(_bf(a), _bf(b), (((1,), (1,)), ((), ())), preferred_element_type=F32)


def _mm_tn(a, b):
    return lax.dot_general(_bf(a), _bf(b), (((0,), (0,)), ((), ())), preferred_element_type=F32)


def _softplus(x):
    return jnp.maximum(x, 0.0) + jnp.log(1.0 + jnp.exp(-jnp.abs(x)))


def _head_sum(x, bd_ones):
    hi = _bf(x)
    lo = _bf(x - hi.astype(F32))
    return (jnp.dot(hi, bd_ones, preferred_element_type=F32)
            + jnp.dot(lo, bd_ones, preferred_element_type=F32))


def _mod_norm(x, g, shift, scale):
    y = x * lax.rsqrt(jnp.mean(x * x, axis=-1, keepdims=True) + EPS) * g
    return y * (1.0 + scale) + shift


def _params(n_axes=1):
    return pltpu.CompilerParams(dimension_semantics=("arbitrary",) * n_axes,
                                vmem_limit_bytes=VMEM_LIMIT)


def _layer_spec(arr, layer):
    rest = arr.shape[1:]
    return pl.BlockSpec((None,) + rest, lambda *_: (layer,) + (0,) * len(rest))


def _full_spec(arr):
    return pl.BlockSpec(arr.shape, lambda *_: (0,) * arr.ndim)


def _row_spec(tm, width, col=0):
    return pl.BlockSpec((tm, width), lambda i: (i, col))


def _cast_kernel(x_ref, *o_refs, splits):
    for o_ref, (lo, hi) in zip(o_refs, splits):
        o_ref[...] = _bf(x_ref[:, lo:hi])


def _cast_bf16(w, splits=None):
    lead, (rows, cols) = w.shape[:-2], w.shape[-2:]
    splits = splits or ((0, cols),)
    n_rows = math.prod(lead) * rows
    tm = 256
    outs = pl.pallas_call(
        functools.partial(_cast_kernel, splits=splits),
        grid=(n_rows // tm,),
        in_specs=[_row_spec(tm, cols)],
        out_specs=[_row_spec(tm, hi - lo) for lo, hi in splits],
        out_shape=[jax.ShapeDtypeStruct((n_rows, hi - lo), BF16) for lo, hi in splits],
        compiler_params=_params(),
        name="cast_bf16",
    )(w.reshape(n_rows, cols))
    return [o.reshape(lead + (rows, hi - lo)) for o, (lo, hi) in zip(outs, splits)]


def _mod_kernel(cond_ref, w_ref, b_ref, o_ref):
    cnd = cond_ref[...]
    act = cnd * jax.nn.sigmoid(cnd)
    o_ref[...] = jnp.dot(act, w_ref[...], preferred_element_type=F32, precision=HIGHEST) + b_ref[...]


def _modulation(cond, w_mod, b_mod):
    depth = w_mod.shape[0]
    n_col = w_mod.shape[2] // D
    return pl.pallas_call(
        _mod_kernel,
        grid=(depth, n_col),
        in_specs=[pl.BlockSpec((SUBLANE_TILE, D), lambda l, j: (0, 0)),
                  pl.BlockSpec((None, D, D), lambda l, j: (l, 0, j)),
                  pl.BlockSpec((None, 1, D), lambda l, j: (l, 0, j))],
        out_specs=pl.BlockSpec((None, SUBLANE_TILE, D), lambda l, j: (l, 0, j)),
        out_shape=jax.ShapeDtypeStruct((depth, SUBLANE_TILE, 6 * D), F32),
        compiler_params=_params(2),
        name="modulation",
    )(cond, w_mod, b_mod.reshape(depth, 1, 6 * D))


def _premix_kernel(h_ref, hp_ref, hn_ref, mod_ref, g_ref, w_ref, cw_ref, cb_ref, mu_ref, kk_ref, bd_ref,
                   xc_ref, ga_ref, zb_ref, uc_ref, kkn_ref, z_scr, *, row, tm):
    i = pl.program_id(0)
    n = pl.num_programs(0)
    halo = SUBLANE_TILE
    shift = mod_ref[row:row + 1, 0:D]
    scale = mod_ref[row:row + 1, D:2 * D]
    hext = jnp.concatenate([hp_ref[...], h_ref[...], hn_ref[...]], axis=0)
    xn = _mod_norm(hext, g_ref[...], shift, scale)
    z_scr[...] = _mm(xn, w_ref[...])

    @pl.when(i == 0)
    def _():
        z_scr[0:halo, :] = jnp.zeros((halo, N_A), F32)

    @pl.when(i == n - 1)
    def _():
        z_scr[tm + halo:tm + 2 * halo, :] = jnp.zeros((halo, N_A), F32)

    acc = cb_ref[...] + cw_ref[0:1, :] * z_scr[halo - 2:halo - 2 + tm, 0:WIDTH]
    for j in range(1, 4):
        acc = acc + cw_ref[j:j + 1, :] * z_scr[halo - 2 + j:halo - 2 + j + tm, 0:WIDTH]
    xc_ref[...] = acc
    ga_ref[...] = z_scr[halo:halo + tm, WIDTH:2 * WIDTH]
    lo, hi = 2 * WIDTH, 2 * WIDTH + RWKV_IN
    zc = z_scr[halo:halo + tm, lo:hi]
    zp = z_scr[halo - 1:halo - 1 + tm, lo:hi]
    zn = z_scr[halo + 1:halo + 1 + tm, lo:hi]
    zs = zc + mu_ref[...] * (0.5 * (zp + zn) - zc)
    zb_ref[...] = zs
    uc_ref[...] = z_scr[halo:halo + tm, hi:hi + WIDTH]
    kk = zs[:, WIDTH:2 * WIDTH] * kk_ref[...]
    ss = _head_sum(kk * kk, bd_ref[...])
    kkn_ref[...] = kk * lax.rsqrt(ss + 1e-12)


def _premix(h, layer, row, mod, norm1, w_a, conv_w, conv_b, mu, kk_w, bd_ones):
    n_tok = h.shape[0]
    tm = min(256, n_tok)
    n = n_tok // tm
    per = tm // SUBLANE_TILE
    last_blk = n_tok // SUBLANE_TILE - 1
    kern = functools.partial(_premix_kernel, row=row, tm=tm)
    outs = pl.pallas_call(
        kern,
        grid=(n,),
        in_specs=[
            _row_spec(tm, D),
            pl.BlockSpec((SUBLANE_TILE, D), lambda i: (jnp.maximum(i * per - 1, 0), 0)),
            pl.BlockSpec((SUBLANE_TILE, D), lambda i: (jnp.minimum((i + 1) * per, last_blk), 0)),
            _layer_spec(mod, layer), _layer_spec(norm1, layer), _layer_spec(w_a, layer),
            _layer_spec(conv_w, layer), _layer_spec(conv_b, layer), _layer_spec(mu, layer),
            _layer_spec(kk_w, layer), _full_spec(bd_ones),
        ],
        out_specs=[_row_spec(tm, WIDTH), _row_spec(tm, WIDTH), _row_spec(tm, RWKV_IN),
                   _row_spec(tm, WIDTH), _row_spec(tm, WIDTH)],
        out_shape=[jax.ShapeDtypeStruct((n_tok, WIDTH), F32), jax.ShapeDtypeStruct((n_tok, WIDTH), F32),
                   jax.ShapeDtypeStruct((n_tok, RWKV_IN), F32), jax.ShapeDtypeStruct((n_tok, WIDTH), F32),
                   jax.ShapeDtypeStruct((n_tok, WIDTH), F32)],
        scratch_shapes=[pltpu.VMEM((tm + 2 * SUBLANE_TILE, N_A), F32)],
        compiler_params=_params(),
        name="premix",
    )(h, h, h, mod, norm1, w_a, conv_w, conv_b, mu, kk_w, bd_ones)
    return outs


def _lru_kernel(xf_ref, xr_ref, wg_ref, bg_ref, lam_ref, h0_ref, hf_ref, hr_ref, fin_ref, carry, *, tm):
    i = pl.program_id(0)

    @pl.when(i == 0)
    def _():
        carry[...] = h0_ref[...]

    rows = lax.broadcasted_iota(jnp.int32, (tm, WIDTH), 0)
    for d, (x_ref, o_ref) in enumerate(((xf_ref, hf_ref), (xr_ref, hr_ref))):
        xc = x_ref[...]
        gates = _mm(xc, wg_ref[d]) + bg_ref[d]
        gate_r = jax.nn.sigmoid(gates[:, 0:WIDTH])
        gate_i = jax.nn.sigmoid(gates[:, WIDTH:2 * WIDTH])
        log_a = -LRU_C * gate_r * _softplus(-lam_ref[d])
        a = jnp.exp(log_a)
        b = jnp.sqrt(1.0 - jnp.exp(2.0 * log_a)) * (gate_i * xc)
        s = 1
        while s < tm:
            if d == 0:
                a_sh, b_sh, m = pltpu.roll(a, s, 0), pltpu.roll(b, s, 0), rows >= s
            else:
                a_sh, b_sh, m = pltpu.roll(a, tm - s, 0), pltpu.roll(b, tm - s, 0), rows < tm - s
            b = jnp.where(m, a * b_sh + b, b)
            a = jnp.where(m, a * a_sh, a)
            s *= 2
        hs = a * carry[d] + b
        o_ref[...] = hs
        carry[d] = hs[tm - 1:tm, :] if d == 0 else hs[0:1, :]
    fin_ref[...] = carry[...]


def _lru(xc, layer, w_gate, b_gate, lam, h0):
    n_tok = xc.shape[0]
    tm = min(256, n_tok)
    n = n_tok // tm
    kern = functools.partial(_lru_kernel, tm=tm)
    return pl.pallas_call(
        kern,
        grid=(n,),
        in_specs=[_row_spec(tm, WIDTH),
                  pl.BlockSpec((tm, WIDTH), lambda i: (n - 1 - i, 0)),
                  _layer_spec(w_gate, layer), _layer_spec(b_gate, layer), _layer_spec(lam, layer),
                  _full_spec(h0)],
        out_specs=[_row_spec(tm, WIDTH),
                   pl.BlockSpec((tm, WIDTH), lambda i: (n - 1 - i, 0)),
                   pl.BlockSpec((2, 1, WIDTH), lambda i: (0, 0, 0))],
        out_shape=[jax.ShapeDtypeStruct((n_tok, WIDTH), F32), jax.ShapeDtypeStruct((n_tok, WIDTH), F32),
                   jax.ShapeDtypeStruct((2, 1, WIDTH), F32)],
        scratch_shapes=[pltpu.VMEM((2, 1, WIDTH), F32)],
        compiler_params=_params(),
        name="rglru_scan",
    )(xc, xc, w_gate, b_gate, lam, h0)


def _block_rows(x, lane_head):
    return jnp.concatenate([jnp.where(lane_head == h, x, 0.0) for h in range(QUAD // HEAD)], axis=0)


def _interleave(gens):
    results = [None] * len(gens)
    active = list(range(len(gens)))
    while active:
        still = []
        for g in active:
            try:
                next(gens[g])
                still.append(g)
            except StopIteration as stop:
                results[g] = stop.value
        active = still
    return results


def _wkv_kernel(zf_ref, zr_ref, kf_ref, kr_ref, w2_ref, a2_ref, w0_ref, a0_ref, ka_ref, rk_ref, bd_ref, s0_ref,
                yf_ref, yr_ref, bf_ref, br_ref, sfin_ref, s_scr, *, n_chunk):
    i = pl.program_id(0)
    t = WKV_CHUNK

    @pl.when(i == 0)
    def _():
        s_scr[...] = s0_ref[...]

    row = lax.broadcasted_iota(jnp.int32, (t, QUAD), 0)
    lane = lax.broadcasted_iota(jnp.int32, (t, QUAD), 1)
    lane_head = lane >> 6
    lane_tok = lane & (HEAD - 1)
    row_t = lax.broadcasted_iota(jnp.int32, (t, t), 0)
    col_t = lax.broadcasted_iota(jnp.int32, (t, t), 1)
    same_head = (lax.broadcasted_iota(jnp.int32, (QUAD, QUAD), 0) >> 6) == (
        lax.broadcasted_iota(jnp.int32, (QUAD, QUAD), 1) >> 6)
    eye = jnp.where(lane_tok == row, 1.0, 0.0)
    pair_mask = [((row ^ lane_tok) >> lvl) == 1 for lvl in range(int(math.log2(t)))]

    n_quad = WIDTH // QUAD
    bd = functools.partial(_block_rows, lane_head=lane_head)

    def chunk_setup(d, at, rt, bt, kt, vq):
        strict, incl = (lane_tok < row, lane_tok <= row) if d == 0 else (lane_tok > row, lane_tok >= row)
        ar = jnp.concatenate([at, rt], axis=0)
        gb = _mm_nt(ar, bd(bt))
        yield
        gk = _mm_nt(ar, bd(kt))
        yield
        a_ab = jnp.where(strict, gb[0:t], 0.0)
        a_rb = jnp.where(incl, gb[t:2 * t], 0.0)
        a_ak = jnp.where(strict, gk[0:t], 0.0)
        a_rk = jnp.where(incl, gk[t:2 * t], 0.0)
        inv = eye + jnp.where(pair_mask[0], a_ab, 0.0)
        for lvl in range(1, len(pair_mask)):
            half = _mm(inv, bd(jnp.where(pair_mask[lvl], a_ab, 0.0)))
            yield
            inv = inv + _mm(half, bd(inv))
            yield
        v_bd = bd(vq)
        xy0 = _mm(jnp.concatenate([a_ak, a_rk], axis=0), v_bd)
        yield
        x0, y0 = xy0[0:t], xy0[t:2 * t]
        wu = _mm(inv, jnp.concatenate([bd(at), bd(x0)], axis=1))
        yield
        return wu[:, 0:QUAD], wu[:, QUAD:2 * QUAD], a_rb, y0

    def state_chain(d, q, order, pre, post, y_ref):
        s = s_scr[d, q]
        for c in order:
            w1, u0, a_rb, y0 = pre[(d, c, q)]
            rt, vq, b_end, k_end, decay = post[(d, c, q)]
            ws = _mm_nt(jnp.concatenate([w1, rt], axis=0), s)
            yield
            u = ws[0:t] + u0
            y = ws[t:2 * t] + y0 + _mm(a_rb, bd(u))
            yield
            upd = _mm_tn(jnp.concatenate([u, vq], axis=0), jnp.concatenate([b_end, k_end], axis=0))
            yield
            s = s * decay + jnp.where(same_head, upd, 0.0)
            y_ref[c * t:(c + 1) * t, q * QUAD:(q + 1) * QUAD] = y
        s_scr[d, q] = s

    setups, post = {}, {}
    dirs = ((zf_ref, kf_ref, yf_ref, bf_ref), (zr_ref, kr_ref, yr_ref, br_ref))
    for d, (z_ref, k_ref, y_ref, b_ref) in enumerate(dirs):
        r = z_ref[:, 0:WIDTH]
        k = z_ref[:, WIDTH:2 * WIDTH]
        v = z_ref[:, 2 * WIDTH:3 * WIDTH]
        wd = z_ref[:, 3 * WIDTH:3 * WIDTH + LANE_TILE]
        ad = z_ref[:, 3 * WIDTH + LANE_TILE:3 * WIDTH + 2 * LANE_TILE]
        kkn = k_ref[...]
        w_log = -_softplus(-(w0_ref[d] + _mm(jnp.tanh(wd), w2_ref[d]))) - 0.5
        lw = -jnp.exp(w_log)
        iclr = jax.nn.sigmoid(a0_ref[d] + _mm(ad, a2_ref[d]))
        kd = k * (1.0 + (iclr - 1.0) * ka_ref[...])
        kb = kkn * iclr
        b_ref[...] = _mm(r * kd * rk_ref[...], bd_ref[...]) * v
        p1 = _bf(lw)
        p2 = _bf(lw - p1.astype(F32))
        lw2 = jnp.concatenate([p1, p2], axis=1)
        tri = jnp.where(col_t <= row_t if d == 0 else col_t >= row_t, 1.0, 0.0).astype(BF16)
        for c in range(n_chunk):
            rows_c = slice(c * t, (c + 1) * t)
            cum2 = jnp.dot(tri, lw2[rows_c], preferred_element_type=F32)
            cum = cum2[:, 0:WIDTH] + cum2[:, WIDTH:2 * WIDTH]
            tot = cum[t - 1:t, :] if d == 0 else cum[0:1, :]
            e_neg = jnp.exp(-cum)
            e_end = jnp.exp(tot - cum)
            a_t = -kkn[rows_c] * jnp.exp(cum - lw[rows_c])
            r_t = r[rows_c] * jnp.exp(cum)
            b_t = kb[rows_c] * e_neg
            k_t = kd[rows_c] * e_neg
            b_end = kb[rows_c] * e_end
            k_end = kd[rows_c] * e_end
            decay_tot = jnp.exp(tot)
            for q in range(n_quad):
                sl = slice(q * QUAD, (q + 1) * QUAD)
                vq = v[rows_c, sl]
                setups[(d, c, q)] = chunk_setup(d, a_t[:, sl], r_t[:, sl], b_t[:, sl], k_t[:, sl], vq)
                post[(d, c, q)] = (r_t[:, sl], vq, b_end[:, sl], k_end[:, sl], decay_tot[:, sl])

    keys = list(setups)
    pre = dict(zip(keys, _interleave([setups[key] for key in keys])))
    chains = []
    for d, y_ref in ((0, yf_ref), (1, yr_ref)):
        order = range(n_chunk) if d == 0 else range(n_chunk - 1, -1, -1)
        chains += [state_chain(d, q, order, pre, post, y_ref) for q in range(n_quad)]
    _interleave(chains)
    sfin_ref[...] = s_scr[...]


def _wkv(zbs, kkn, layer, w2p, a2p, w0, a0, ka, rk, bd_ones, s0):
    n_tok = zbs.shape[0]
    t = min(256, n_tok)
    n = n_tok // t
    fwd = lambda i: (i, 0)
    rev = lambda i: (n - 1 - i, 0)
    state_shape = s0.shape
    return pl.pallas_call(
        functools.partial(_wkv_kernel, n_chunk=t // WKV_CHUNK),
        grid=(n,),
        in_specs=[pl.BlockSpec((t, RWKV_IN), fwd), pl.BlockSpec((t, RWKV_IN), rev),
                  pl.BlockSpec((t, WIDTH), fwd), pl.BlockSpec((t, WIDTH), rev),
                  _layer_spec(w2p, layer), _layer_spec(a2p, layer), _layer_spec(w0, layer),
                  _layer_spec(a0, layer), _layer_spec(ka, layer), _layer_spec(rk, layer),
                  _full_spec(bd_ones), _full_spec(s0)],
        out_specs=[pl.BlockSpec((t, WIDTH), fwd), pl.BlockSpec((t, WIDTH), rev),
                   pl.BlockSpec((t, WIDTH), fwd), pl.BlockSpec((t, WIDTH), rev),
                   pl.BlockSpec(state_shape, lambda i: (0, 0, 0, 0))],
        out_shape=[jax.ShapeDtypeStruct((n_tok, WIDTH), F32)] * 4 + [jax.ShapeDtypeStruct(state_shape, F32)],
        scratch_shapes=[pltpu.VMEM(state_shape, F32)],
        compiler_params=_params(),
        name="wkv7_scan",
    )(zbs, zbs, kkn, kkn, w2p, a2p, w0, a0, ka, rk, bd_ones, s0)


def _cmul_rows(x, p):
    half = LANE_TILE // 2
    lane = lax.broadcasted_iota(jnp.int32, p.shape, 1)
    p_sw = pltpu.roll(p, half, 1)
    p1 = jnp.where(lane < half, p, p_sw)[0:1, :]
    p2 = jnp.where(lane < half, -p_sw, p)[0:1, :]
    return x * p1 + pltpu.roll(x, half, 1) * p2


def _s5_kernel(u_ref, w_ref, v_ref, lam_ref, h0_ref, y_ref, fin_ref, *, nc):
    blk = S5_CHUNK * S5_GROUP
    res = _mm(u_ref[...], w_ref[...])
    row = lax.broadcasted_iota(jnp.int32, (nc, LANE_TILE), 0)
    h_in = []
    for d in range(2):
        e = res[:, blk + d * LANE_TILE:blk + (d + 1) * LANE_TILE]
        lam = jnp.broadcast_to(lam_ref[d], (SUBLANE_TILE, LANE_TILE))
        h0 = jnp.broadcast_to(h0_ref[d], (SUBLANE_TILE, LANE_TILE))
        first = 0 if d == 0 else nc - 1
        e = jnp.where(row == first, e + _cmul_rows(h0, lam)[0:1, :], e)
        p = lam
        s = 1
        while s < nc:
            if d == 0:
                sh, m = pltpu.roll(e, s, 0), row >= s
            else:
                sh, m = pltpu.roll(e, nc - s, 0), row < nc - s
            e = e + jnp.where(m, _cmul_rows(sh, p), 0.0)
            p = _cmul_rows(p, p)
            s *= 2
        if d == 0:
            fin_ref[d] = e[nc - 1:nc, :]
            h_in.append(jnp.where(row >= 1, pltpu.roll(e, 1, 0), h0[0:1, :]))
        else:
            fin_ref[d] = e[0:1, :]
            h_in.append(jnp.where(row < nc - 1, pltpu.roll(e, nc - 1, 0), h0[0:1, :]))
    y_ref[...] = _bf(res[:, 0:blk] + _mm(jnp.concatenate(h_in, axis=1), v_ref[...]))


def _s5(u_blocks, layer, w_cat, v_cat, lam16, h0):
    groups, nc, blk = u_blocks.shape
    kern = functools.partial(_s5_kernel, nc=nc)
    return pl.pallas_call(
        kern,
        grid=(groups,),
        in_specs=[pl.BlockSpec((None, nc, blk), lambda g: (g, 0, 0)),
                  pl.BlockSpec((None, None, blk, w_cat.shape[-1]), lambda g: (layer, g, 0, 0)),
                  pl.BlockSpec((None, None, blk, blk), lambda g: (layer, g, 0, 0)),
                  pl.BlockSpec((None, 2, None, 1, LANE_TILE), lambda g: (layer, 0, g, 0, 0)),
                  pl.BlockSpec((2, None, 1, LANE_TILE), lambda g: (0, g, 0, 0))],
        out_specs=[pl.BlockSpec((None, nc, blk), lambda g: (g, 0, 0)),
                   pl.BlockSpec((2, None, 1, LANE_TILE), lambda g: (0, g, 0, 0))],
        out_shape=[jax.ShapeDtypeStruct((groups, nc, blk), BF16),
                   jax.ShapeDtypeStruct((2, groups, 1, LANE_TILE), F32)],
        compiler_params=_params(),
        name="s5_scan",
    )(u_blocks, w_cat, v_cat, lam16, h0)


def _s5_weights(lam_re, lam_im, log_step, b_re, b_im, c_re, c_im):
    n = S5_CHUNK
    step = jnp.exp(log_step)[..., None]
    x_re, ang = lam_re * step, lam_im * step
    mag = jnp.exp(x_re)
    lb_re, lb_im = mag * jnp.cos(ang), mag * jnp.sin(ang)
    nr = jnp.expm1(x_re) * jnp.cos(ang) - 2.0 * jnp.square(jnp.sin(0.5 * ang))
    den = lam_re * lam_re + lam_im * lam_im
    f_re = (nr * lam_re + lb_im * lam_im) / den
    f_im = (lb_im * lam_re - nr * lam_im) / den
    bb_re = f_re[..., None] * b_re - f_im[..., None] * b_im
    bb_im = f_re[..., None] * b_im + f_im[..., None] * b_re
    pr, pi = [jnp.ones_like(lb_re)], [jnp.zeros_like(lb_re)]
    for _ in range(n):
        pr, pi = pr + [pr[-1] * lb_re - pi[-1] * lb_im], pi + [pr[-1] * lb_im + pi[-1] * lb_re]
    pw_re, pw_im = jnp.stack(pr, axis=-2), jnp.stack(pi, axis=-2)
    cp_re = c_re[..., None, :, :] * pw_re[..., :, None, :] - c_im[..., None, :, :] * pw_im[..., :, None, :]
    cp_im = c_re[..., None, :, :] * pw_im[..., :, None, :] + c_im[..., None, :, :] * pw_re[..., :, None, :]
    taps = jnp.sum(cp_re[..., :, None] * bb_re[..., None, None, :, :]
                   - cp_im[..., :, None] * bb_im[..., None, None, :, :], axis=-2)
    fwd, bwd = taps[:, 0], taps[:, 1]
    both = jnp.concatenate([jnp.flip(bwd[..., 1:n, :, :], axis=-3), fwd[..., 0:1, :, :] + bwd[..., 0:1, :, :],
                            fwd[..., 1:n, :, :]], axis=-3)
    lag = jnp.arange(n)[None, :] - jnp.arange(n)[:, None] + (n - 1)
    m_tot = jnp.take(_bf(both), lag.reshape(-1), axis=-3)
    m_tot = m_tot.reshape(both.shape[:-3] + (n, n, S5_GROUP, S5_GROUP))
    m_tot = jnp.swapaxes(jnp.swapaxes(m_tot, -1, -2), -2, -3)
    m_tot = m_tot.reshape(both.shape[:-3] + (n * S5_GROUP, n * S5_GROUP))

    def end_state(d, order):
        qr, qi = pw_re[:, d][..., order, :], pw_im[:, d][..., order, :]
        er = qr[..., :, None, :] * jnp.swapaxes(bb_re[:, d], -1, -2)[..., None, :, :] \
            - qi[..., :, None, :] * jnp.swapaxes(bb_im[:, d], -1, -2)[..., None, :, :]
        ei = qr[..., :, None, :] * jnp.swapaxes(bb_im[:, d], -1, -2)[..., None, :, :] \
            + qi[..., :, None, :] * jnp.swapaxes(bb_re[:, d], -1, -2)[..., None, :, :]
        e = jnp.concatenate([er, ei], axis=-1)
        return e.reshape(e.shape[:-3] + (n * S5_GROUP, 2 * S5_STATE))

    def state_out(d, order):
        vr = jnp.moveaxis(cp_re[:, d][..., order, :, :], -1, -3)
        vi = -jnp.moveaxis(cp_im[:, d][..., order, :, :], -1, -3)
        vv = jnp.concatenate([vr, vi], axis=-3)
        return vv.reshape(vv.shape[:-3] + (2 * S5_STATE, n * S5_GROUP))

    asc = jnp.arange(n)
    w_cat = jnp.concatenate([m_tot, _bf(end_state(0, n - 1 - asc)), _bf(end_state(1, asc))], axis=-1)
    v_cat = jnp.concatenate([state_out(0, asc + 1), state_out(1, n - asc)], axis=-2)
    lam16 = jnp.concatenate([pw_re[..., n, :], pw_im[..., n, :]], axis=-1)[..., None, :]
    return _bf(w_cat), _bf(v_cat), lam16


def _merge_kernel(h_ref, hf_ref, hr_ref, ga_ref, yf_ref, yr_ref, bf_ref, br_ref, gd_ref, ys_ref, uc_ref,
                  mod_ref, g_ref, wzg_ref, lnw_ref, lnb_ref, g2_ref, dsk_ref, wglu_ref, bglu_ref,
                  wbr_ref, wout_ref, bd_ref, o_ref, *, row):
    h = h_ref[...]
    shift = mod_ref[row:row + 1, 0:D]
    scale = mod_ref[row:row + 1, D:2 * D]
    gate = mod_ref[row:row + 1, 2 * D:3 * D]
    zg = _mm(_mod_norm(h, g_ref[...], shift, scale), wzg_ref[...])
    bd = bd_ref[...]
    y_a = jax.nn.gelu(ga_ref[...]) * (hf_ref[...] + hr_ref[...])
    y = yf_ref[...] + yr_ref[...]
    inv_n = 1.0 / HEAD
    yc = y - _mm(y, bd) * inv_n
    var = _mm(yc * yc, bd) * inv_n
    y_b = yc * lax.rsqrt(var + GN_EPS) * lnw_ref[...] + lnb_ref[...] + (bf_ref[...] + br_ref[...])
    y_b = y_b * _mm(jax.nn.sigmoid(gd_ref[...]), g2_ref[...])
    y_s = jax.nn.gelu(ys_ref[...].astype(F32) + dsk_ref[...] * uc_ref[...])
    y_c = y_s * jax.nn.sigmoid(_mm(y_s, wglu_ref[...]) + bglu_ref[...])
    mix = (jax.nn.sigmoid(zg[:, 0:D]) * _mm(y_a, wbr_ref[0])
           + jax.nn.sigmoid(zg[:, D:2 * D]) * _mm(y_b, wbr_ref[1])
           + jax.nn.sigmoid(zg[:, 2 * D:3 * D]) * _mm(y_c, wbr_ref[2]))
    o_ref[...] = h + gate * _mm(mix, wout_ref[...])


def _merge(h, parts, layer, row, mod, norm1, w_zg, lnw, lnb, g2, dsk, w_glu, b_glu, w_branch, w_out, bd_ones):
    n_tok = h.shape[0]
    tm = min(256, n_tok)
    hf, hr, ga, yf, yr, bf_, br_, zbs, ys5, uc = parts
    kern = functools.partial(_merge_kernel, row=row)
    rs = _row_spec(tm, WIDTH)
    gd_col = (3 * WIDTH + 2 * LANE_TILE) // LANE_TILE
    weights = (mod, norm1, w_zg, lnw, lnb, g2, dsk, w_glu, b_glu, w_branch, w_out)
    return pl.pallas_call(
        kern,
        grid=(n_tok // tm,),
        in_specs=[_row_spec(tm, D), rs, rs, rs, rs, rs, rs, rs, _row_spec(tm, LANE_TILE, gd_col), rs, rs]
        + [_layer_spec(w, layer) for w in weights] + [_full_spec(bd_ones)],
        out_specs=_row_spec(tm, D),
        out_shape=jax.ShapeDtypeStruct((n_tok, D), F32),
        compiler_params=_params(),
        name="merge_project",
    )(h, hf, hr, ga, yf, yr, bf_, br_, zbs, ys5, uc, *weights, bd_ones)


def _ffn_kernel(h_ref, mod_ref, g_ref, win_ref, wout_ref, o_ref, *, row):
    h = h_ref[...]
    shift = mod_ref[row:row + 1, 3 * D:4 * D]
    scale = mod_ref[row:row + 1, 4 * D:5 * D]
    gate = mod_ref[row:row + 1, 5 * D:6 * D]
    gu = _mm(_mod_norm(h, g_ref[...], shift, scale), win_ref[...])
    a, up = gu[:, 0:FFN_HIDDEN], gu[:, FFN_HIDDEN:2 * FFN_HIDDEN]
    o_ref[...] = h + gate * _mm(a * jax.nn.sigmoid(a) * up, wout_ref[...])


def _ffn(h, layer, row, mod, norm2, w_in, w_out):
    n_tok = h.shape[0]
    tm = min(256, n_tok)
    kern = functools.partial(_ffn_kernel, row=row)
    return pl.pallas_call(
        kern,
        grid=(n_tok // tm,),
        in_specs=[_row_spec(tm, D), _layer_spec(mod, layer), _layer_spec(norm2, layer),
                  _layer_spec(w_in, layer), _layer_spec(w_out, layer)],
        out_specs=_row_spec(tm, D),
        out_shape=jax.ShapeDtypeStruct((n_tok, D), F32),
        compiler_params=_params(),
        name="swiglu",
    )(h, mod, norm2, w_in, w_out)


def _final_norm_kernel(h_ref, g_ref, o_ref):
    x = h_ref[...]
    o_ref[...] = x * lax.rsqrt(jnp.mean(x * x, axis=-1, keepdims=True) + EPS) * g_ref[...]


def _final_norm(h, g):
    n_tok = h.shape[0]
    tm = min(512, n_tok)
    return pl.pallas_call(
        _final_norm_kernel,
        grid=(n_tok // tm,),
        in_specs=[_row_spec(tm, D), _full_spec(g)],
        out_specs=_row_spec(tm, D),
        out_shape=jax.ShapeDtypeStruct((n_tok, D), F32),
        compiler_params=_params(),
        name="final_norm",
    )(h, g)


def _block_diag(w):
    n_h, a, b = w.shape[-3:]
    out = w[..., :, :, None, :] * jnp.eye(n_h, dtype=w.dtype)[:, None, :, None]
    return out.reshape(w.shape[:-3] + (n_h * a, n_h * b))


def _pad_lora(w):
    z = jnp.zeros_like(w[:, 0])
    return jnp.stack([jnp.concatenate([w[:, 0], z], axis=1), jnp.concatenate([z, w[:, 1]], axis=1)], axis=1)


def _to_s5_blocks(u, col_major):
    n_tok = u.shape[0]
    if col_major:
        rows = n_tok // GRID_W
        x = u.reshape(rows // S5_CHUNK, S5_CHUNK, GRID_W, S5_GROUPS, S5_GROUP)
        x = x.transpose(3, 2, 0, 1, 4)
    else:
        x = u.reshape(n_tok // S5_CHUNK, S5_CHUNK, S5_GROUPS, S5_GROUP).transpose(2, 0, 1, 3)
    return x.reshape(S5_GROUPS, n_tok // S5_CHUNK, S5_CHUNK * S5_GROUP)


def _from_s5_blocks(y, n_tok, col_major):
    if col_major:
        rows = n_tok // GRID_W
        x = y.reshape(S5_GROUPS, GRID_W, rows // S5_CHUNK, S5_CHUNK, S5_GROUP).transpose(2, 3, 1, 0, 4)
    else:
        x = y.reshape(S5_GROUPS, n_tok // S5_CHUNK, S5_CHUNK, S5_GROUP).transpose(1, 2, 0, 3)
    return x.reshape(n_tok, WIDTH)


def _prepare(c, c_ctx, w_mod, b_mod, norm1, norm2, w_in, lru_conv_w, lru_conv_b, lru_wa, lru_ba, lru_wx, lru_bx,
             lru_lam, rwkv_mu, rwkv_w0, rwkv_w2, rwkv_a0, rwkv_a2, rwkv_g2, rwkv_kk, rwkv_ka, rwkv_rk, rwkv_lnw,
             rwkv_lnb, s5_lam_re, s5_lam_im, s5_log_step, s5_b_re, s5_b_im, s5_c_re, s5_c_im, s5_d, s5_w_glu,
             s5_b_glu, w_branch, w_out, w_ffn_in, w_ffn_out):
    depth = w_in.shape[0]
    vec = lambda a: a.reshape(depth, 1, a.shape[-1])
    cond = jnp.concatenate([c, c_ctx[None], jnp.zeros((SUBLANE_TILE - 2, D), F32)], axis=0)
    w_a, w_zg = _cast_bf16(w_in, ((0, N_A), (N_A, w_in.shape[-1])))
    s5_w, s5_v, s5_lam16 = _s5_weights(s5_lam_re, s5_lam_im, s5_log_step, s5_b_re, s5_b_im, s5_c_re, s5_c_im)
    head_id = jnp.arange(WIDTH) // HEAD
    return dict(
        mod=_modulation(cond, w_mod, b_mod),
        w_a=w_a, w_zg=w_zg,
        w_gate=_bf(jnp.concatenate([_block_diag(lru_wa), _block_diag(lru_wx)], axis=-1)),
        b_gate=jnp.concatenate([lru_ba, lru_bx], axis=-1)[:, :, None, :],
        lam=lru_lam[:, :, None, :],
        w2p=_bf(_pad_lora(rwkv_w2)), a2p=_bf(_pad_lora(rwkv_a2)),
        w0=rwkv_w0[:, :, None, :], a0=rwkv_a0[:, :, None, :], rk=rwkv_rk.reshape(depth, 1, WIDTH),
        s5_w=s5_w, s5_v=s5_v, s5_lam16=s5_lam16,
        w_branch=_cast_bf16(w_branch)[0], w_out=_cast_bf16(w_out)[0],
        w_ffn_in=_cast_bf16(w_ffn_in)[0], w_ffn_out=_cast_bf16(w_ffn_out)[0],
        g2=_bf(rwkv_g2), w_glu=_bf(s5_w_glu), n1=vec(norm1), n2=vec(norm2), conv_w=lru_conv_w,
        conv_b=vec(lru_conv_b), mu=vec(rwkv_mu), kk_w=vec(rwkv_kk), ka=vec(rwkv_ka), lnw=vec(rwkv_lnw),
        lnb=vec(rwkv_lnb), dsk=vec(s5_d), b_glu=vec(s5_b_glu),
        bd_ones=(head_id[:, None] == head_id[None, :]).astype(BF16),
    )


def _zero_states():
    return (jnp.zeros((2, 1, WIDTH), F32),
            jnp.zeros((2, WIDTH // QUAD, QUAD, QUAD), F32),
            jnp.zeros((2, S5_GROUPS, 1, LANE_TILE), F32))


def _mixers(p, h, layer, row, states, col_major):
    n_tok = h.shape[0]
    xc, ga, zbs, uc, kkn = _premix(h, layer, row, p["mod"], p["n1"], p["w_a"], p["conv_w"], p["conv_b"],
                                   p["mu"], p["kk_w"], p["bd_ones"])
    hf, hr, fin_lru = _lru(xc, layer, p["w_gate"], p["b_gate"], p["lam"], states[0])
    yf, yr, bf_, br_, fin_wkv = _wkv(zbs, kkn, layer, p["w2p"], p["a2p"], p["w0"], p["a0"], p["ka"], p["rk"],
                                     p["bd_ones"], states[1])
    ys5, fin_s5 = _s5(_to_s5_blocks(_bf(uc), col_major), layer, p["s5_w"], p["s5_v"], p["s5_lam16"], states[2])
    ys5 = _from_s5_blocks(ys5, n_tok, col_major)
    return (hf, hr, ga, yf, yr, bf_, br_, zbs, ys5, uc), (fin_lru, fin_wkv, fin_s5)


def _channel_mix(p, h, parts, layer, row):
    h = _merge(h, parts, layer, row, p["mod"], p["n1"], p["w_zg"], p["lnw"], p["lnb"], p["g2"], p["dsk"],
               p["w_glu"], p["b_glu"], p["w_branch"], p["w_out"], p["bd_ones"])
    return _ffn(h, layer, row, p["mod"], p["n2"], p["w_ffn_in"], p["w_ffn_out"])


def kernel(x, c, ctx, c_ctx, w_mod, b_mod, norm1, norm2, norm_f, w_in, lru_conv_w, lru_conv_b, lru_wa, lru_ba,
           lru_wx, lru_bx, lru_lam, rwkv_mu, rwkv_w0, rwkv_w2, rwkv_a0, rwkv_a2, rwkv_g2, rwkv_kk, rwkv_ka,
           rwkv_rk, rwkv_lnw, rwkv_lnb, s5_lam_re, s5_lam_im, s5_log_step, s5_b_re, s5_b_im, s5_c_re, s5_c_im,
           s5_d, s5_w_glu, s5_b_glu, w_branch, w_out, w_ffn_in, w_ffn_out):
    bsz, n_lat, d_model = x.shape
    n_ctx = ctx.shape[1]
    depth = w_in.shape[0]
    assert bsz == 1 and d_model == D
    assert n_lat % (GRID_W * S5_CHUNK) == 0 and n_lat % 256 == 0
    assert n_ctx % WKV_CHUNK == 0 and (n_ctx <= 256 or n_ctx % 256 == 0)
    p = _prepare(c, c_ctx, w_mod, b_mod, norm1, norm2, w_in, lru_conv_w, lru_conv_b, lru_wa, lru_ba, lru_wx,
                 lru_bx, lru_lam, rwkv_mu, rwkv_w0, rwkv_w2, rwkv_a0, rwkv_a2, rwkv_g2, rwkv_kk, rwkv_ka, rwkv_rk,
                 rwkv_lnw, rwkv_lnb, s5_lam_re, s5_lam_im, s5_log_step, s5_b_re, s5_b_im, s5_c_re, s5_c_im, s5_d,
                 s5_w_glu, s5_b_glu, w_branch, w_out, w_ffn_in, w_ffn_out)
    h_lat, h_ctx = x[0], ctx[0]
    for layer in range(depth):
        parts_c, states = _mixers(p, h_ctx, layer, 1, _zero_states(), False)
        parts_l, _ = _mixers(p, h_lat, layer, 0, states, True)
        h_lat = _channel_mix(p, h_lat, parts_l, layer, 0)
        if layer != depth - 1:
            h_ctx = _channel_mix(p, h_ctx, parts_c, layer, 1)
    return _final_norm(h_lat, norm_f.reshape(1, D))[None]
```

```python
import functools
import math

import jax
import jax.numpy as jnp
from jax import lax
from jax.experimental import pallas as pl
from jax.experimental.pallas import tpu as pltpu

F32 = jnp.float32
BF16 = jnp.bfloat16
HIGHEST = lax.Precision.HIGHEST

D = 1024
WIDTH = 512
GRID_W = 64
EPS = 1e-6
LRU_C = 8.0
HEAD = 64
RWKV_IN = 3 * WIDTH + 2 * 64 + 2 * 64 + 128
GN_EPS = 64e-5
S5_GROUPS = 32
S5_GROUP = 16
S5_STATE = 64
S5_CHUNK = 16
N_A = 2 * WIDTH + RWKV_IN + WIDTH
FFN_HIDDEN = 2816
WKV_CHUNK = 64
LANE_TILE = 128
SUBLANE_TILE = 8
QUAD = 4 * HEAD
VMEM_LIMIT = 56 * 1024 * 1024


def _bf(x):
    return x.astype(BF16)


def _mm(a, b):
    return jnp.dot(_bf(a), _bf(b), preferred_element_type=F32)


def _mm_nt(a, b):
    return lax.dot_general(_bf(a), _bf(b), (((1,), (1,)), ((), ())), preferred_element_type=F32)


def _mm_tn(a, b):
    return lax.dot_general(_bf(a), _bf(b), (((0,), (0,)), ((), ())), preferred_element_type=F32)


def _softplus(x):
    return jnp.maximum(x, 0.0) + jnp.log(1.0 + jnp.exp(-jnp.abs(x)))


def _head_sum(x, bd_ones):
    hi = _bf(x)
    lo = _bf(x - hi.astype(F32))
    return (jnp.dot(hi, bd_ones, preferred_element_type=F32)
            + jnp.dot(lo, bd_ones, preferred_element_type=F32))


def _mod_norm(x, g, shift, scale):
    y = x * lax.rsqrt(jnp.mean(x * x, axis=-1, keepdims=True) + EPS) * g
    return y * (1.0 + scale) + shift


def _params(n_axes=1):
    return pltpu.CompilerParams(dimension_semantics=("arbitrary",) * n_axes,
                                vmem_limit_bytes=VMEM_LIMIT)


def _layer_spec(arr, layer):
    rest = arr.shape[1:]
    return pl.BlockSpec((None,) + rest, lambda *_: (layer,) + (0,) * len(rest))


def _full_spec(arr):
    return pl.BlockSpec(arr.shape, lambda *_: (0,) * arr.ndim)


def _row_spec(tm, width, col=0):
    return pl.BlockSpec((tm, width), lambda i: (i, col))


def _cast_kernel(x_ref, *o_refs, splits):
    for o_ref, (lo, hi) in zip(o_refs, splits):
        o_ref[...] = _bf(x_ref[:, lo:hi])


def _cast_bf16(w, splits=None):
    lead, (rows, cols) = w.shape[:-2], w.shape[-2:]
    splits = splits or ((0, cols),)
    n_rows = math.prod(lead) * rows
    tm = 256
    outs = pl.pallas_call(
        functools.partial(_cast_kernel, splits=splits),
        grid=(n_rows // tm,),
        in_specs=[_row_spec(tm, cols)],
        out_specs=[_row_spec(tm, hi - lo) for lo, hi in splits],
        out_shape=[jax.ShapeDtypeStruct((n_rows, hi - lo), BF16) for lo, hi in splits],
        compiler_params=_params(),
        name="cast_bf16",
    )(w.reshape(n_rows, cols))
    return [o.reshape(lead + (rows, hi - lo)) for o, (lo, hi) in zip(outs, splits)]


def _mod_kernel(cond_ref, w_ref, b_ref, o_ref):
    cnd = cond_ref[...]
    act = cnd * jax.nn.sigmoid(cnd)
    o_ref[...] = jnp.dot(act, w_ref[...], preferred_element_type=F32, precision=HIGHEST) + b_ref[...]


def _modulation(cond, w_mod, b_mod):
    depth = w_mod.shape[0]
    n_col = w_mod.shape[2] // D
    return pl.pallas_call(
        _mod_kernel,
        grid=(depth, n_col),
        in_specs=[pl.BlockSpec((SUBLANE_TILE, D), lambda l, j: (0, 0)),
                  pl.BlockSpec((None, D, D), lambda l, j: (l, 0, j)),
                  pl.BlockSpec((None, 1, D), lambda l, j: (l, 0, j))],
        out_specs=pl.BlockSpec((None, SUBLANE_TILE, D), lambda l, j: (l, 0, j)),
        out_shape=jax.ShapeDtypeStruct((depth, SUBLANE_TILE, 6 * D), F32),
        compiler_params=_params(2),
        name="modulation",
    )(cond, w_mod, b_mod.reshape(depth, 1, 6 * D))


def _premix_kernel(h_ref, hp_ref, hn_ref, mod_ref, g_ref, w_ref, cw_ref, cb_ref, mu_ref, kk_ref, bd_ref,
                   xc_ref, ga_ref, zb_ref, uc_ref, kkn_ref, z_scr, *, row, tm):
    i = pl.program_id(0)
    n = pl.num_programs(0)
    halo = SUBLANE_TILE
    shift = mod_ref[row:row + 1, 0:D]
    scale = mod_ref[row:row + 1, D:2 * D]
    hext = jnp.concatenate([hp_ref[...], h_ref[...], hn_ref[...]], axis=0)
    xn = _mod_norm(hext, g_ref[...], shift, scale)
    z_scr[...] = _mm(xn, w_ref[...])

    @pl.when(i == 0)
    def _():
        z_scr[0:halo, :] = jnp.zeros((halo, N_A), F32)

    @pl.when(i == n - 1)
    def _():
        z_scr[tm + halo:tm + 2 * halo, :] = jnp.zeros((halo, N_A), F32)

    acc = cb_ref[...] + cw_ref[0:1, :] * z_scr[halo - 2:halo - 2 + tm, 0:WIDTH]
    for j in range(1, 4):
        acc = acc + cw_ref[j:j + 1, :] * z_scr[halo - 2 + j:halo - 2 + j + tm, 0:WIDTH]
    xc_ref[...] = acc
    ga_ref[...] = z_scr[halo:halo + tm, WIDTH:2 * WIDTH]
    lo, hi = 2 * WIDTH, 2 * WIDTH + RWKV_IN
    zc = z_scr[halo:halo + tm, lo:hi]
    zp = z_scr[halo - 1:halo - 1 + tm, lo:hi]
    zn = z_scr[halo + 1:halo + 1 + tm, lo:hi]
    zs = zc + mu_ref[...] * (0.5 * (zp + zn) - zc)
    zb_ref[...] = zs
    uc_ref[...] = z_scr[halo:halo + tm, hi:hi + WIDTH]
    kk = zs[:, WIDTH:2 * WIDTH] * kk_ref[...]
    ss = _head_sum(kk * kk, bd_ref[...])
    kkn_ref[...] = kk * lax.rsqrt(ss + 1e-12)


def _premix(h, layer, row, mod, norm1, w_a, conv_w, conv_b, mu, kk_w, bd_ones):
    n_tok = h.shape[0]
    tm = min(256, n_tok)
    n = n_tok // tm
    per = tm // SUBLANE_TILE
    last_blk = n_tok // SUBLANE_TILE - 1
    kern = functools.partial(_premix_kernel, row=row, tm=tm)
    outs = pl.pallas_call(
        kern,
        grid=(n,),
        in_specs=[
            _row_spec(tm, D),
            pl.BlockSpec((SUBLANE_TILE, D), lambda i: (jnp.maximum(i * per - 1, 0), 0)),
            pl.BlockSpec((SUBLANE_TILE, D), lambda i: (jnp.minimum((i + 1) * per, last_blk), 0)),
            _layer_spec(mod, layer), _layer_spec(norm1, layer), _layer_spec(w_a, layer),
            _layer_spec(conv_w, layer), _layer_spec(conv_b, layer), _layer_spec(mu, layer),
            _layer_spec(kk_w, layer), _full_spec(bd_ones),
        ],
        out_specs=[_row_spec(tm, WIDTH), _row_spec(tm, WIDTH), _row_spec(tm, RWKV_IN),
                   _row_spec(tm, WIDTH), _row_spec(tm, WIDTH)],
        out_shape=[jax.ShapeDtypeStruct((n_tok, WIDTH), F32), jax.ShapeDtypeStruct((n_tok, WIDTH), F32),
                   jax.ShapeDtypeStruct((n_tok, RWKV_IN), F32), jax.ShapeDtypeStruct((n_tok, WIDTH), F32),
                   jax.ShapeDtypeStruct((n_tok, WIDTH), F32)],
        scratch_shapes=[pltpu.VMEM((tm + 2 * SUBLANE_TILE, N_A), F32)],
        compiler_params=_params(),
        name="premix",
    )(h, h, h, mod, norm1, w_a, conv_w, conv_b, mu, kk_w, bd_ones)
    return outs


def _lru_kernel(xf_ref, xr_ref, wg_ref, bg_ref, lam_ref, h0_ref, hf_ref, hr_ref, fin_ref, carry, *, tm):
    i = pl.program_id(0)

    @pl.when(i == 0)
    def _():
        carry[...] = h0_ref[...]

    rows = lax.broadcasted_iota(jnp.int32, (tm, WIDTH), 0)
    for d, (x_ref, o_ref) in enumerate(((xf_ref, hf_ref), (xr_ref, hr_ref))):
        xc = x_ref[...]
        gates = _mm(xc, wg_ref[d]) + bg_ref[d]
        gate_r = jax.nn.sigmoid(gates[:, 0:WIDTH])
        gate_i = jax.nn.sigmoid(gates[:, WIDTH:2 * WIDTH])
        log_a = -LRU_C * gate_r * _softplus(-lam_ref[d])
        a = jnp.exp(log_a)
        b = jnp.sqrt(1.0 - jnp.exp(2.0 * log_a)) * (gate_i * xc)
        s = 1
        while s < tm:
            if d == 0:
                a_sh, b_sh, m = pltpu.roll(a, s, 0), pltpu.roll(b, s, 0), rows >= s
            else:
                a_sh, b_sh, m = pltpu.roll(a, tm - s, 0), pltpu.roll(b, tm - s, 0), rows < tm - s
            b = jnp.where(m, a * b_sh + b, b)
            a = jnp.where(m, a * a_sh, a)
            s *= 2
        hs = a * carry[d] + b
        o_ref[...] = hs
        carry[d] = hs[tm - 1:tm, :] if d == 0 else hs[0:1, :]
    fin_ref[...] = carry[...]


def _lru(xc, layer, w_gate, b_gate, lam, h0):
    n_tok = xc.shape[0]
    tm = min(256, n_tok)
    n = n_tok // tm
    kern = functools.partial(_lru_kernel, tm=tm)
    return pl.pallas_call(
        kern,
        grid=(n,),
        in_specs=[_row_spec(tm, WIDTH),
                  pl.BlockSpec((tm, WIDTH), lambda i: (n - 1 - i, 0)),
                  _layer_spec(w_gate, layer), _layer_spec(b_gate, layer), _layer_spec(lam, layer),
                  _full_spec(h0)],
        out_specs=[_row_spec(tm, WIDTH),
                   pl.BlockSpec((tm, WIDTH), lambda i: (n - 1 - i, 0)),
                   pl.BlockSpec((2, 1, WIDTH), lambda i: (0, 0, 0))],
        out_shape=[jax.ShapeDtypeStruct((n_tok, WIDTH), F32), jax.ShapeDtypeStruct((n_tok, WIDTH), F32),
                   jax.ShapeDtypeStruct((2, 1, WIDTH), F32)],
        scratch_shapes=[pltpu.VMEM((2, 1, WIDTH), F32)],
        compiler_params=_params(),
        name="rglru_scan",
    )(xc, xc, w_gate, b_gate, lam, h0)


def _block_rows(x, lane_head):
    return jnp.concatenate([jnp.where(lane_head == h, x, 0.0) for h in range(QUAD // HEAD)], axis=0)


def _interleave(gens):
    results = [None] * len(gens)
    active = list(range(len(gens)))
    while active:
        still = []
        for g in active:
            try:
                next(gens[g])
                still.append(g)
            except StopIteration as stop:
                results[g] = stop.value
        active = still
    return results


def _wkv_kernel(zf_ref, zr_ref, kf_ref, kr_ref, w2_ref, a2_ref, w0_ref, a0_ref, ka_ref, rk_ref, bd_ref, s0_ref,
                yf_ref, yr_ref, bf_ref, br_ref, sfin_ref, s_scr, *, n_chunk):
    i = pl.program_id(0)
    t = WKV_CHUNK

    @pl.when(i == 0)
    def _():
        s_scr[...] = s0_ref[...]

    row = lax.broadcasted_iota(jnp.int32, (t, QUAD), 0)
    lane = lax.broadcasted_iota(jnp.int32, (t, QUAD), 1)
    lane_head = lane >> 6
    lane_tok = lane & (HEAD - 1)
    row_t = lax.broadcasted_iota(jnp.int32, (t, t), 0)
    col_t = lax.broadcasted_iota(jnp.int32, (t, t), 1)
    same_head = (lax.broadcasted_iota(jnp.int32, (QUAD, QUAD), 0) >> 6) == (
        lax.broadcasted_iota(jnp.int32, (QUAD, QUAD), 1) >> 6)
    eye = jnp.where(lane_tok == row, 1.0, 0.0)
    pair_mask = [((row ^ lane_tok) >> lvl) == 1 for lvl in range(int(math.log2(t)))]

    n_quad = WIDTH // QUAD
    bd = functools.partial(_block_rows, lane_head=lane_head)

    def chunk_setup(d, at, rt, bt, kt, vq):
        strict, incl = (lane_tok < row, lane_tok <= row) if d == 0 else (lane_tok > row, lane_tok >= row)
        ar = jnp.concatenate([at, rt], axis=0)
        gb = _mm_nt(ar, bd(bt))
        yield
        gk = _mm_nt(ar, bd(kt))
        yield
        a_ab = jnp.where(strict, gb[0:t], 0.0)
        a_rb = jnp.where(incl, gb[t:2 * t], 0.0)
        a_ak = jnp.where(strict, gk[0:t], 0.0)
        a_rk = jnp.where(incl, gk[t:2 * t], 0.0)
        inv = eye + jnp.where(pair_mask[0], a_ab, 0.0)
        for lvl in range(1, len(pair_mask)):
            half = _mm(inv, bd(jnp.where(pair_mask[lvl], a_ab, 0.0)))
            yield
            inv = inv + _mm(half, bd(inv))
            yield
        v_bd = bd(vq)
        xy0 = _mm(jnp.concatenate([a_ak, a_rk], axis=0), v_bd)
        yield
        x0, y0 = xy0[0:t], xy0[t:2 * t]
        wu = _mm(inv, jnp.concatenate([bd(at), bd(x0)], axis=1))
        yield
        return wu[:, 0:QUAD], wu[:, QUAD:2 * QUAD], a_rb, y0

    def state_chain(d, q, order, pre, post, y_ref):
        s = s_scr[d, q]
        for c in order:
            w1, u0, a_rb, y0 = pre[(d, c, q)]
            rt, vq, b_end, k_end, decay = post[(d, c, q)]
            ws = _mm_nt(jnp.concatenate([w1, rt], axis=0), s)
            yield
            u = ws[0:t] + u0
            y = ws[t:2 * t] + y0 + _mm(a_rb, bd(u))
            yield
            upd = _mm_tn(jnp.concatenate([u, vq], axis=0), jnp.concatenate([b_end, k_end], axis=0))
            yield
            s = s * decay + jnp.where(same_head, upd, 0.0)
            y_ref[c * t:(c + 1) * t, q * QUAD:(q + 1) * QUAD] = y
        s_scr[d, q] = s

    setups, post = {}, {}
    dirs = ((zf_ref, kf_ref, yf_ref, bf_ref), (zr_ref, kr_ref, yr_ref, br_ref))
    for d, (z_ref, k_ref, y_ref, b_ref) in enumerate(dirs):
        r = z_ref[:, 0:WIDTH]
        k = z_ref[:, WIDTH:2 * WIDTH]
        v = z_ref[:, 2 * WIDTH:3 * WIDTH]
        wd = z_ref[:, 3 * WIDTH:3 * WIDTH + LANE_TILE]
        ad = z_ref[:, 3 * WIDTH + LANE_TILE:3 * WIDTH + 2 * LANE_TILE]
        kkn = k_ref[...]
        w_log = -_softplus(-(w0_ref[d] + _mm(jnp.tanh(wd), w2_ref[d]))) - 0.5
        lw = -jnp.exp(w_log)
        iclr = jax.nn.sigmoid(a0_ref[d] + _mm(ad, a2_ref[d]))
        kd = k * (1.0 + (iclr - 1.0) * ka_ref[...])
        kb = kkn * iclr
        b_ref[...] = _mm(r * kd * rk_ref[...], bd_ref[...]) * v
        p1 = _bf(lw)
        p2 = _bf(lw - p1.astype(F32))
        lw2 = jnp.concatenate([p1, p2], axis=1)
        tri = jnp.where(col_t <= row_t if d == 0 else col_t >= row_t, 1.0, 0.0).astype(BF16)
        for c in range(n_chunk):
            rows_c = slice(c * t, (c + 1) * t)
            cum2 = jnp.dot(tri, lw2[rows_c], preferred_element_type=F32)
            cum = cum2[:, 0:WIDTH] + cum2[:, WIDTH:2 * WIDTH]
            tot = cum[t - 1:t, :] if d == 0 else cum[0:1, :]
            e_neg = jnp.exp(-cum)
            e_end = jnp.exp(tot - cum)
            a_t = -kkn[rows_c] * jnp.exp(cum - lw[rows_c])
            r_t = r[rows_c] * jnp.exp(cum)
            b_t = kb[rows_c] * e_neg
            k_t = kd[rows_c] * e_neg
            b_end = kb[rows_c] * e_end
            k_end = kd[rows_c] * e_end
            decay_tot = jnp.exp(tot)
            for q in range(n_quad):
                sl = slice(q * QUAD, (q + 1) * QUAD)
                vq = v[rows_c, sl]
                setups[(d, c, q)] = chunk_setup(d, a_t[:, sl], r_t[:, sl], b_t[:, sl], k_t[:, sl], vq)
                post[(d, c, q)] = (r_t[:, sl], vq, b_end[:, sl], k_end[:, sl], decay_tot[:, sl])

    keys = list(setups)
    pre = dict(zip(keys, _interleave([setups[key] for key in keys])))
    chains = []
    for d, y_ref in ((0, yf_ref), (1, yr_ref)):
        order = range(n_chunk) if d == 0 else range(n_chunk - 1, -1, -1)
        chains += [state_chain(d, q, order, pre, post, y_ref) for q in range(n_quad)]
    _interleave(chains)
    sfin_ref[...] = s_scr[...]


def _wkv(zbs, kkn, layer, w2p, a2p, w0, a0, ka, rk, bd_ones, s0):
    n_tok = zbs.shape[0]
    t = min(256, n_tok)
    n = n_tok // t
    fwd = lambda i: (i, 0)
    rev = lambda i: (n - 1 - i, 0)
    state_shape = s0.shape
    return pl.pallas_call(
        functools.partial(_wkv_kernel, n_chunk=t // WKV_CHUNK),
        grid=(n,),
        in_specs=[pl.BlockSpec((t, RWKV_IN), fwd), pl.BlockSpec((t, RWKV_IN), rev),
                  pl.BlockSpec((t, WIDTH), fwd), pl.BlockSpec((t, WIDTH), rev),
                  _layer_spec(w2p, layer), _layer_spec(a2p, layer), _layer_spec(w0, layer),
                  _layer_spec(a0, layer), _layer_spec(ka, layer), _layer_spec(rk, layer),
                  _full_spec(bd_ones), _full_spec(s0)],
        out_specs=[pl.BlockSpec((t, WIDTH), fwd), pl.BlockSpec((t, WIDTH), rev),
                   pl.BlockSpec((t, WIDTH), fwd), pl.BlockSpec((t, WIDTH), rev),
                   pl.BlockSpec(state_shape, lambda i: (0, 0, 0, 0))],
        out_shape=[jax.ShapeDtypeStruct((n_tok, WIDTH), F32)] * 4 + [jax.ShapeDtypeStruct(state_shape, F32)],
        scratch_shapes=[pltpu.VMEM(state_shape, F32)],
        compiler_params=_params(),
        name="wkv7_scan",
    )(zbs, zbs, kkn, kkn, w2p, a2p, w0, a0, ka, rk, bd_ones, s0)


def _cmul_rows(x, p):
    half = LANE_TILE // 2
    lane = lax.broadcasted_iota(jnp.int32, p.shape, 1)
    p_sw = pltpu.roll(p, half, 1)
    p1 = jnp.where(lane < half, p, p_sw)[0:1, :]
    p2 = jnp.where(lane < half, -p_sw, p)[0:1, :]
    return x * p1 + pltpu.roll(x, half, 1) * p2


def _s5_kernel(u_ref, w_ref, v_ref, lam_ref, h0_ref, y_ref, fin_ref, *, nc):
    blk = S5_CHUNK * S5_GROUP
    res = _mm(u_ref[...], w_ref[...])
    row = lax.broadcasted_iota(jnp.int32, (nc, LANE_TILE), 0)
    h_in = []
    for d in range(2):
        e = res[:, blk + d * LANE_TILE:blk + (d + 1) * LANE_TILE]
        lam = jnp.broadcast_to(lam_ref[d], (SUBLANE_TILE, LANE_TILE))
        h0 = jnp.broadcast_to(h0_ref[d], (SUBLANE_TILE, LANE_TILE))
        first = 0 if d == 0 else nc - 1
        e = jnp.where(row == first, e + _cmul_rows(h0, lam)[0:1, :], e)
        p = lam
        s = 1
        while s < nc:
            if d == 0:
                sh, m = pltpu.roll(e, s, 0), row >= s
            else:
                sh, m = pltpu.roll(e, nc - s, 0), row < nc - s
            e = e + jnp.where(m, _cmul_rows(sh, p), 0.0)
            p = _cmul_rows(p, p)
            s *= 2
        if d == 0:
            fin_ref[d] = e[nc - 1:nc, :]
            h_in.append(jnp.where(row >= 1, pltpu.roll(e, 1, 0), h0[0:1, :]))
        else:
            fin_ref[d] = e[0:1, :]
            h_in.append(jnp.where(row < nc - 1, pltpu.roll(e, nc - 1, 0), h0[0:1, :]))
    y_ref[...] = _bf(res[:, 0:blk] + _mm(jnp.concatenate(h_in, axis=1), v_ref[...]))


def _s5(u_blocks, layer, w_cat, v_cat, lam16, h0):
    groups, nc, blk = u_blocks.shape
    kern = functools.partial(_s5_kernel, nc=nc)
    return pl.pallas_call(
        kern,
        grid=(groups,),
        in_specs=[pl.BlockSpec((None, nc, blk), lambda g: (g, 0, 0)),
                  pl.BlockSpec((None, None, blk, w_cat.shape[-1]), lambda g: (layer, g, 0, 0)),
                  pl.BlockSpec((None, None, blk, blk), lambda g: (layer, g, 0, 0)),
                  pl.BlockSpec((None, 2, None, 1, LANE_TILE), lambda g: (layer, 0, g, 0, 0)),
                  pl.BlockSpec((2, None, 1, LANE_TILE), lambda g: (0, g, 0, 0))],
        out_specs=[pl.BlockSpec((None, nc, blk), lambda g: (g, 0, 0)),
                   pl.BlockSpec((2, None, 1, LANE_TILE), lambda g: (0, g, 0, 0))],
        out_shape=[jax.ShapeDtypeStruct((groups, nc, blk), BF16),
                   jax.ShapeDtypeStruct((2, groups, 1, LANE_TILE), F32)],
        compiler_params=_params(),
        name="s5_scan",
    )(u_blocks, w_cat, v_cat, lam16, h0)


def _s5_weights(lam_re, lam_im, log_step, b_re, b_im, c_re, c_im):
    n = S5_CHUNK
    step = jnp.exp(log_step)[..., None]
    x_re, ang = lam_re * step, lam_im * step
    mag = jnp.exp(x_re)
    lb_re, lb_im = mag * jnp.cos(ang), mag * jnp.sin(ang)
    nr = jnp.expm1(x_re) * jnp.cos(ang) - 2.0 * jnp.square(jnp.sin(0.5 * ang))
    den = lam_re * lam_re + lam_im * lam_im
    f_re = (nr * lam_re + lb_im * lam_im) / den
    f_im = (lb_im * lam_re - nr * lam_im) / den
    bb_re = f_re[..., None] * b_re - f_im[..., None] * b_im
    bb_im = f_re[..., None] * b_im + f_im[..., None] * b_re
    pr, pi = [jnp.ones_like(lb_re)], [jnp.zeros_like(lb_re)]
    for _ in range(n):
        pr, pi = pr + [pr[-1] * lb_re - pi[-1] * lb_im], pi + [pr[-1] * lb_im + pi[-1] * lb_re]
    pw_re, pw_im = jnp.stack(pr, axis=-2), jnp.stack(pi, axis=-2)
    cp_re = c_re[..., None, :, :] * pw_re[..., :, None, :] - c_im[..., None, :, :] * pw_im[..., :, None, :]
    cp_im = c_re[..., None, :, :] * pw_im[..., :, None, :] + c_im[..., None, :, :] * pw_re[..., :, None, :]
    taps = jnp.einsum('...jcp,...pk->...jck', _bf(jnp.concatenate([cp_re, -cp_im], axis=-1)),
                      _bf(jnp.concatenate([bb_re, bb_im], axis=-2)), preferred_element_type=F32)
    fwd, bwd = taps[:, 0], taps[:, 1]
    both = jnp.concatenate([jnp.flip(bwd[..., 1:n, :, :], axis=-3), fwd[..., 0:1, :, :] + bwd[..., 0:1, :, :],
                            fwd[..., 1:n, :, :]], axis=-3)
    lag = jnp.arange(n)[None, :] - jnp.arange(n)[:, None] + (n - 1)
    m_tot = jnp.take(_bf(both), lag.reshape(-1), axis=-3)
    m_tot = m_tot.reshape(both.shape[:-3] + (n, n, S5_GROUP, S5_GROUP))
    m_tot = jnp.swapaxes(jnp.swapaxes(m_tot, -1, -2), -2, -3)
    m_tot = m_tot.reshape(both.shape[:-3] + (n * S5_GROUP, n * S5_GROUP))

    def end_state(d, order):
        qr, qi = pw_re[:, d][..., order, :], pw_im[:, d][..., order, :]
        er = qr[..., :, None, :] * jnp.swapaxes(bb_re[:, d], -1, -2)[..., None, :, :] \
            - qi[..., :, None, :] * jnp.swapaxes(bb_im[:, d], -1, -2)[..., None, :, :]
        ei = qr[..., :, None, :] * jnp.swapaxes(bb_im[:, d], -1, -2)[..., None, :, :] \
            + qi[..., :, None, :] * jnp.swapaxes(bb_re[:, d], -1, -2)[..., None, :, :]
        e = jnp.concatenate([er, ei], axis=-1)
        return e.reshape(e.shape[:-3] + (n * S5_GROUP, 2 * S5_STATE))

    def state_out(d, order):
        vr = jnp.moveaxis(cp_re[:, d][..., order, :, :], -1, -3)
        vi = -jnp.moveaxis(cp_im[:, d][..., order, :, :], -1, -3)
        vv = jnp.concatenate([vr, vi], axis=-3)
        return vv.reshape(vv.shape[:-3] + (2 * S5_STATE, n * S5_GROUP))

    asc = jnp.arange(n)
    w_cat = jnp.concatenate([m_tot, _bf(end_state(0, n - 1 - asc)), _bf(end_state(1, asc))], axis=-1)
    v_cat = jnp.concatenate([state_out(0, asc + 1), state_out(1, n - asc)], axis=-2)
    lam16 = jnp.concatenate([pw_re[..., n, :], pw_im[..., n, :]], axis=-1)[..., None, :]
    return _bf(w_cat), _bf(v_cat), lam16


def _merge_kernel(h_ref, hf_ref, hr_ref, ga_ref, yf_ref, yr_ref, bf_ref, br_ref, gd_ref, ys_ref, uc_ref,
                  mod_ref, g_ref, wzg_ref, lnw_ref, lnb_ref, g2_ref, dsk_ref, wglu_ref, bglu_ref,
                  wbr_ref, wout_ref, bd_ref, o_ref, *, row):
    h = h_ref[...]
    shift = mod_ref[row:row + 1, 0:D]
    scale = mod_ref[row:row + 1, D:2 * D]
    gate = mod_ref[row:row + 1, 2 * D:3 * D]
    zg = _mm(_mod_norm(h, g_ref[...], shift, scale), wzg_ref[...])
    bd = bd_ref[...]
    y_a = jax.nn.gelu(ga_ref[...]) * (hf_ref[...] + hr_ref[...])
    y = yf_ref[...] + yr_ref[...]
    inv_n = 1.0 / HEAD
    yc = y - _mm(y, bd) * inv_n
    var = _mm(yc * yc, bd) * inv_n
    y_b = yc * lax.rsqrt(var + GN_EPS) * lnw_ref[...] + lnb_ref[...] + (bf_ref[...] + br_ref[...])
    y_b = y_b * _mm(jax.nn.sigmoid(gd_ref[...]), g2_ref[...])
    y_s = jax.nn.gelu(ys_ref[...].astype(F32) + dsk_ref[...] * uc_ref[...])
    y_c = y_s * jax.nn.sigmoid(_mm(y_s, wglu_ref[...]) + bglu_ref[...])
    mix = (jax.nn.sigmoid(zg[:, 0:D]) * _mm(y_a, wbr_ref[0])
           + jax.nn.sigmoid(zg[:, D:2 * D]) * _mm(y_b, wbr_ref[1])
           + jax.nn.sigmoid(zg[:, 2 * D:3 * D]) * _mm(y_c, wbr_ref[2]))
    o_ref[...] = h + gate * _mm(mix, wout_ref[...])


def _merge(h, parts, layer, row, mod, norm1, w_zg, lnw, lnb, g2, dsk, w_glu, b_glu, w_branch, w_out, bd_ones):
    n_tok = h.shape[0]
    tm = min(256, n_tok)
    hf, hr, ga, yf, yr, bf_, br_, zbs, ys5, uc = parts
    kern = functools.partial(_merge_kernel, row=row)
    rs = _row_spec(tm, WIDTH)
    gd_col = (3 * WIDTH + 2 * LANE_TILE) // LANE_TILE
    weights = (mod, norm1, w_zg, lnw, lnb, g2, dsk, w_glu, b_glu, w_branch, w_out)
    return pl.pallas_call(
        kern,
        grid=(n_tok // tm,),
        in_specs=[_row_spec(tm, D), rs, rs, rs, rs, rs, rs, rs, _row_spec(tm, LANE_TILE, gd_col), rs, rs]
        + [_layer_spec(w, layer) for w in weights] + [_full_spec(bd_ones)],
        out_specs=_row_spec(tm, D),
        out_shape=jax.ShapeDtypeStruct((n_tok, D), F32),
        compiler_params=_params(),
        name="merge_project",
    )(h, hf, hr, ga, yf, yr, bf_, br_, zbs, ys5, uc, *weights, bd_ones)


def _ffn_kernel(h_ref, mod_ref, g_ref, win_ref, wout_ref, o_ref, *, row):
    h = h_ref[...]
    shift = mod_ref[row:row + 1, 3 * D:4 * D]
    scale = mod_ref[row:row + 1, 4 * D:5 * D]
    gate = mod_ref[row:row + 1, 5 * D:6 * D]
    gu = _mm(_mod_norm(h, g_ref[...], shift, scale), win_ref[...])
    a, up = gu[:, 0:FFN_HIDDEN], gu[:, FFN_HIDDEN:2 * FFN_HIDDEN]
    o_ref[...] = h + gate * _mm(a * jax.nn.sigmoid(a) * up, wout_ref[...])


def _ffn(h, layer, row, mod, norm2, w_in, w_out):
    n_tok = h.shape[0]
    tm = min(256, n_tok)
    kern = functools.partial(_ffn_kernel, row=row)
    return pl.pallas_call(
        kern,
        grid=(n_tok // tm,),
        in_specs=[_row_spec(tm, D), _layer_spec(mod, layer), _layer_spec(norm2, layer),
                  _layer_spec(w_in, layer), _layer_spec(w_out, layer)],
        out_specs=_row_spec(tm, D),
        out_shape=jax.ShapeDtypeStruct((n_tok, D), F32),
        compiler_params=_params(),
        name="swiglu",
    )(h, mod, norm2, w_in, w_out)


def _final_norm_kernel(h_ref, g_ref, o_ref):
    x = h_ref[...]
    o_ref[...] = x * lax.rsqrt(jnp.mean(x * x, axis=-1, keepdims=True) + EPS) * g_ref[...]


def _final_norm(h, g):
    n_tok = h.shape[0]
    tm = min(512, n_tok)
    return pl.pallas_call(
        _final_norm_kernel,
        grid=(n_tok // tm,),
        in_specs=[_row_spec(tm, D), _full_spec(g)],
        out_specs=_row_spec(tm, D),
        out_shape=jax.ShapeDtypeStruct((n_tok, D), F32),
        compiler_params=_params(),
        name="final_norm",
    )(h, g)


def _block_diag(w):
    n_h, a, b = w.shape[-3:]
    out = w[..., :, :, None, :] * jnp.eye(n_h, dtype=w.dtype)[:, None, :, None]
    return out.reshape(w.shape[:-3] + (n_h * a, n_h * b))


def _pad_lora(w):
    z = jnp.zeros_like(w[:, 0])
    return jnp.stack([jnp.concatenate([w[:, 0], z], axis=1), jnp.concatenate([z, w[:, 1]], axis=1)], axis=1)


def _to_s5_blocks(u, col_major):
    n_tok = u.shape[0]
    if col_major:
        rows = n_tok // GRID_W
        x = u.reshape(rows // S5_CHUNK, S5_CHUNK, GRID_W, S5_GROUPS, S5_GROUP)
        x = x.transpose(3, 2, 0, 1, 4)
    else:
        x = u.reshape(n_tok // S5_CHUNK, S5_CHUNK, S5_GROUPS, S5_GROUP).transpose(2, 0, 1, 3)
    return x.reshape(S5_GROUPS, n_tok // S5_CHUNK, S5_CHUNK * S5_GROUP)


def _from_s5_blocks(y, n_tok, col_major):
    if col_major:
        rows = n_tok // GRID_W
        x = y.reshape(S5_GROUPS, GRID_W, rows // S5_CHUNK, S5_CHUNK, S5_GROUP).transpose(2, 3, 1, 0, 4)
    else:
        x = y.reshape(S5_GROUPS, n_tok // S5_CHUNK, S5_CHUNK, S5_GROUP).transpose(1, 2, 0, 3)
    return x.reshape(n_tok, WIDTH)


def _prepare(c, c_ctx, w_mod, b_mod, norm1, norm2, w_in, lru_conv_w, lru_conv_b, lru_wa, lru_ba, lru_wx, lru_bx,
             lru_lam, rwkv_mu, rwkv_w0, rwkv_w2, rwkv_a0, rwkv_a2, rwkv_g2, rwkv_kk, rwkv_ka, rwkv_rk, rwkv_lnw,
             rwkv_lnb, s5_lam_re, s5_lam_im, s5_log_step, s5_b_re, s5_b_im, s5_c_re, s5_c_im, s5_d, s5_w_glu,
             s5_b_glu, w_branch, w_out, w_ffn_in, w_ffn_out):
    depth = w_in.shape[0]
    vec = lambda a: a.reshape(depth, 1, a.shape[-1])
    cond = jnp.concatenate([c, c_ctx[None], jnp.zeros((SUBLANE_TILE - 2, D), F32)], axis=0)
    w_a, w_zg = _cast_bf16(w_in, ((0, N_A), (N_A, w_in.shape[-1])))
    s5_w, s5_v, s5_lam16 = _s5_weights(s5_lam_re, s5_lam_im, s5_log_step, s5_b_re, s5_b_im, s5_c_re, s5_c_im)
    head_id = jnp.arange(WIDTH) // HEAD
    return dict(
        mod=_modulation(cond, w_mod, b_mod),
        w_a=w_a, w_zg=w_zg,
        w_gate=_bf(jnp.concatenate([_block_diag(lru_wa), _block_diag(lru_wx)], axis=-1)),
        b_gate=jnp.concatenate([lru_ba, lru_bx], axis=-1)[:, :, None, :],
        lam=lru_lam[:, :, None, :],
        w2p=_bf(_pad_lora(rwkv_w2)), a2p=_bf(_pad_lora(rwkv_a2)),
        w0=rwkv_w0[:, :, None, :], a0=rwkv_a0[:, :, None, :], rk=rwkv_rk.reshape(depth, 1, WIDTH),
        s5_w=s5_w, s5_v=s5_v, s5_lam16=s5_lam16,
        w_branch=_cast_bf16(w_branch)[0], w_out=_cast_bf16(w_out)[0],
        w_ffn_in=_cast_bf16(w_ffn_in)[0], w_ffn_out=_cast_bf16(w_ffn_out)[0],
        g2=_bf(rwkv_g2), w_glu=_bf(s5_w_glu), n1=vec(norm1), n2=vec(norm2), conv_w=lru_conv_w,
        conv_b=vec(lru_conv_b), mu=vec(rwkv_mu), kk_w=vec(rwkv_kk), ka=vec(rwkv_ka), lnw=vec(rwkv_lnw),
        lnb=vec(rwkv_lnb), dsk=vec(s5_d), b_glu=vec(s5_b_glu),
        bd_ones=(head_id[:, None] == head_id[None, :]).astype(BF16),
    )


def _zero_states():
    return (jnp.zeros((2, 1, WIDTH), F32),
            jnp.zeros((2, WIDTH // QUAD, QUAD, QUAD), F32),
            jnp.zeros((2, S5_GROUPS, 1, LANE_TILE), F32))


def _mixers(p, h, layer, row, states, col_major):
    n_tok = h.shape[0]
    xc, ga, zbs, uc, kkn = _premix(h, layer, row, p["mod"], p["n1"], p["w_a"], p["conv_w"], p["conv_b"],
                                   p["mu"], p["kk_w"], p["bd_ones"])
    hf, hr, fin_lru = _lru(xc, layer, p["w_gate"], p["b_gate"], p["lam"], states[0])
    yf, yr, bf_, br_, fin_wkv = _wkv(zbs, kkn, layer, p["w2p"], p["a2p"], p["w0"], p["a0"], p["ka"], p["rk"],
                                     p["bd_ones"], states[1])
    ys5, fin_s5 = _s5(_to_s5_blocks(_bf(uc), col_major), layer, p["s5_w"], p["s5_v"], p["s5_lam16"], states[2])
    ys5 = _from_s5_blocks(ys5, n_tok, col_major)
    return (hf, hr, ga, yf, yr, bf_, br_, zbs, ys5, uc), (fin_lru, fin_wkv, fin_s5)


def _channel_mix(p, h, parts, layer, row):
    h = _merge(h, parts, layer, row, p["mod"], p["n1"], p["w_zg"], p["lnw"], p["lnb"], p["g2"], p["dsk"],
               p["w_glu"], p["b_glu"], p["w_branch"], p["w_out"], p["bd_ones"])
    return _ffn(h, layer, row, p["mod"], p["n2"], p["w_ffn_in"], p["w_ffn_out"])


def kernel(x, c, ctx, c_ctx, w_mod, b_mod, norm1, norm2, norm_f, w_in, lru_conv_w, lru_conv_b, lru_wa, lru_ba,
           lru_wx, lru_bx, lru_lam, rwkv_mu, rwkv_w0, rwkv_w2, rwkv_a0, rwkv_a2, rwkv_g2, rwkv_kk, rwkv_ka,
           rwkv_rk, rwkv_lnw, rwkv_lnb, s5_lam_re, s5_lam_im, s5_log_step, s5_b_re, s5_b_im, s5_c_re, s5_c_im,
           s5_d, s5_w_glu, s5_b_glu, w_branch, w_out, w_ffn_in, w_ffn_out):
    bsz, n_lat, d_model = x.shape
    n_ctx = ctx.shape[1]
    depth = w_in.shape[0]
    assert bsz == 1 and d_model == D
    assert n_lat % (GRID_W * S5_CHUNK) == 0 and n_lat % 256 == 0
    assert n_ctx % WKV_CHUNK == 0 and (n_ctx <= 256 or n_ctx % 256 == 0)
    p = _prepare(c, c_ctx, w_mod, b_mod, norm1, norm2, w_in, lru_conv_w, lru_conv_b, lru_wa, lru_ba, lru_wx,
                 lru_bx, lru_lam, rwkv_mu, rwkv_w0, rwkv_w2, rwkv_a0, rwkv_a2, rwkv_g2, rwkv_kk, rwkv_ka, rwkv_rk,
                 rwkv_lnw, rwkv_lnb, s5_lam_re, s5_lam_im, s5_log_step, s5_b_re, s5_b_im, s5_c_re, s5_c_im, s5_d,
                 s5_w_glu, s5_b_glu, w_branch, w_out, w_ffn_in, w_ffn_out)
    h_lat, h_ctx = x[0], ctx[0]
    for layer in range(depth):
        parts_c, states = _mixers(p, h_ctx, layer, 1, _zero_states(), False)
        parts_l, _ = _mixers(p, h_lat, layer, 0, states, True)
        h_lat = _channel_mix(p, h_lat, parts_l, layer, 0)
        if layer != depth - 1:
            h_ctx = _channel_mix(p, h_ctx, parts_c, layer, 1)
    return _final_norm(h_lat, norm_f.reshape(1, D))[None]
```
